```python
import jax, jax.numpy as jnp
from jax import lax
import numpy as np

D_MODEL = 1024
BATCH = 2
SEQ = 16384
DEPTH = 2
DEC_BATCH = 32
DEC_SEQ = 64
PAST_LEN = 1024

CHUNK = 64
HEAD_DIM = 64
GROUP_WIDTH = D_MODEL // 4
MIX_WIDTH = 4 * GROUP_WIDTH
N_OUT_HEADS = MIX_WIDTH // HEAD_DIM
Q_BLOCK = 128
EPS = 1e-6
NEG = -1e30
A_HEADS = GROUP_WIDTH // HEAD_DIM
A_PREV_CHUNKS = 8
A_WIN = A_PREV_CHUNKS * CHUNK
A_BAND = (A_PREV_CHUNKS + 1) * CHUNK
REL_CLIP = 256
B_HEADS = 4
B_NOPE = 64
B_ROPE = 32
B_V = GROUP_WIDTH // B_HEADS
Q_LORA = D_MODEL // 4
KV_LORA = D_MODEL // 8
ROPE_THETA = 10000.0
MLA_SCALE = (B_NOPE + B_ROPE) ** -0.5
C_HEADS = GROUP_WIDTH // HEAD_DIM
M_HEADS = 4
N_MEM = 256

IN_SPLITS = ((GROUP_WIDTH,) * 4 + (Q_LORA, KV_LORA, B_ROPE, GROUP_WIDTH)
             + (GROUP_WIDTH,) * 4 + (GROUP_WIDTH,) * 2)
IN_WIDTH = sum(IN_SPLITS)

kernel_name = 'hybrid_streaming_encoder_step'


def rms(x, g):
    xf = x.astype(jnp.float32)
    y = xf * lax.rsqrt(jnp.mean(xf * xf, -1, keepdims=True) + EPS)
    return (y * g.astype(jnp.float32)).astype(x.dtype)


def heads(x, h):
    return x.reshape(x.shape[:-1] + (h, x.shape[-1] // h))


def split_in(h):
    return jnp.split(h, [int(i) for i in np.cumsum(IN_SPLITS)[:-1]], axis=-1)


def rope(x, pos):
    half = x.shape[-1] // 2
    freqs = ROPE_THETA ** (-jnp.arange(half, dtype=jnp.float32) / half)
    ang = pos.astype(jnp.float32)[:, None] * freqs[None, :]
    cos, sin = jnp.cos(ang)[:, None, :], jnp.sin(ang)[:, None, :]
    x1, x2 = x[..., :half].astype(jnp.float32), x[..., half:].astype(jnp.float32)
    return jnp.concatenate([x1 * cos - x2 * sin, x2 * cos + x1 * sin], -1).astype(x.dtype)


def rel_bias(tab, rel):
    return tab[:, jnp.clip(rel, -REL_CLIP, REL_CLIP) + REL_CLIP].astype(jnp.float32)


def dense_attn(q, k, v, scale):
    s = jnp.einsum('bqhd,bkhd->bhqk', q, k).astype(jnp.float32) * scale
    p = jax.nn.softmax(s, axis=-1)
    return jnp.einsum('bhqk,bkhd->bqhd', p.astype(v.dtype), v)


def band_attn_prompt(q, k, v, tab):
    B, S, H, Dh = q.shape
    nc = S // CHUNK
    qc = q.reshape(B, nc, CHUNK, H, Dh)

    def band(t):
        tp = jnp.concatenate([jnp.zeros((B, A_WIN, H, Dh), t.dtype), t], 1)
        tp = tp.reshape(B, nc + A_PREV_CHUNKS, CHUNK, H, Dh)
        return jnp.concatenate([tp[:, i:i + nc] for i in range(A_PREV_CHUNKS + 1)], 2)

    kb, vb = band(k), band(v)
    rel = (A_WIN + jnp.arange(CHUNK))[:, None] - jnp.arange(A_BAND)[None, :]
    kpos = jnp.arange(nc)[:, None] * CHUNK - A_WIN + jnp.arange(A_BAND)[None, :]
    s = jnp.einsum('bnqhd,bnkhd->bnhqk', qc, kb).astype(jnp.float32) * Dh ** -0.5 + rel_bias(tab, rel)
    s = jnp.where((kpos >= 0)[None, :, None, None, :], s, NEG)
    p = jax.nn.softmax(s, axis=-1)
    o = jnp.einsum('bnhqk,bnkhd->bnqhd', p.astype(v.dtype), vb)
    return o.reshape(B, S, H, Dh)


def band_attn_step(q, k, v, tab):
    T, Dh = q.shape[1], q.shape[-1]
    n_past = k.shape[1] - T
    rel = (n_past + jnp.arange(T))[:, None] - jnp.arange(n_past + T)[None, :]
    s = jnp.einsum('bqhd,bkhd->bhqk', q, k).astype(jnp.float32) * Dh ** -0.5 + rel_bias(tab, rel)
    p = jax.nn.softmax(s, axis=-1)
    return jnp.einsum('bhqk,bkhd->bqhd', p.astype(v.dtype), v)


def chunk_causal_attn_blocks(q, k, v, scale):
    B, S, H, Dq = q.shape
    nb = S // Q_BLOCK
    qb = q.reshape(B, nb, Q_BLOCK, H, Dq).transpose(1, 0, 2, 3, 4)
    kchunk = jnp.arange(S) // CHUNK

    def one(args):
        qi, b = args
        qchunk = (b * Q_BLOCK + jnp.arange(Q_BLOCK)) // CHUNK
        s = jnp.einsum('bqhd,bkhd->bhqk', qi, k).astype(jnp.float32) * scale
        s = jnp.where(kchunk[None, :] <= qchunk[:, None], s, NEG)
        p = jax.nn.softmax(s, axis=-1)
        return jnp.einsum('bhqk,bkhd->bqhd', p.astype(v.dtype), v)

    o = lax.map(one, (qb, jnp.arange(nb)))
    return o.transpose(1, 0, 2, 3, 4).reshape(B, S, H, v.shape[-1])


def stick_break(q, k, v, qpos, kpos):
    Dh = q.shape[-1]
    z = jnp.einsum('bqhd,bkhd->bhqk', q, k).astype(jnp.float32) * Dh ** -0.5
    mask = kpos[None, :] < qpos[:, None]
    log_1m = jnp.where(mask, jax.nn.log_sigmoid(-z), 0.0)
    tail = lax.cumsum(log_1m, axis=3, reverse=True) - log_1m
    w = jnp.where(mask, jnp.exp(jax.nn.log_sigmoid(z) + tail), 0.0)
    return jnp.einsum('bhqk,bkhd->bqhd', w.astype(v.dtype), v)


def stick_break_blocks(q, k, v):
    B, S, H, Dh = q.shape
    nb = S // Q_BLOCK
    qb = q.reshape(B, nb, Q_BLOCK, H, Dh).transpose(1, 0, 2, 3, 4)
    kpos = jnp.arange(S)

    def one(args):
        qi, b = args
        return stick_break(qi, k, v, b * Q_BLOCK + jnp.arange(Q_BLOCK), kpos)

    o = lax.map(one, (qb, jnp.arange(nb)))
    return o.transpose(1, 0, 2, 3, 4).reshape(B, S, H, Dh)


def branch_inputs(x, lp, pos):
    xn = rms(x, lp['norm_g'])
    aq, ak, av, ag, bcq, bckv, bkr, bg, cq, ck, cv, cg, mq, mg = split_in(xn @ lp['w_in'])
    qb = heads(rms(bcq, lp['b_cq_g']) @ lp['b_wq_b'], B_HEADS)
    q_nope = rms(qb[..., :B_NOPE], lp['b_qn_g'])
    q_rope = rope(rms(qb[..., B_NOPE:], lp['b_qr_g']), pos)
    return dict(
        aq=rms(heads(aq, A_HEADS), lp['a_qn_g']),
        ak=rms(heads(ak, A_HEADS), lp['a_kn_g']),
        av=heads(av, A_HEADS),
        bq=jnp.concatenate([q_nope, q_rope], -1),
        ckv=rms(bckv, lp['b_ckv_g']),
        krope=rope(rms(bkr, lp['b_kr_g'])[:, :, None, :], pos)[:, :, 0, :],
        cq=heads(cq, C_HEADS), ck=heads(ck, C_HEADS), cv=heads(cv, C_HEADS),
        mq=rms(heads(mq, M_HEADS), lp['m_qn_g']),
        gates=jnp.concatenate([ag, bg, cg, mg], -1),
    )


def mla_keys(ckv, krope, lp):
    kv = heads(ckv @ lp['b_wkv_b'], B_HEADS)
    k_nope = rms(kv[..., :B_NOPE], lp['b_kn_g'])
    kr = jnp.broadcast_to(krope[:, :, None, :], krope.shape[:2] + (B_HEADS, B_ROPE))
    return jnp.concatenate([k_nope, kr], -1), kv[..., B_NOPE:]


def mem_kv(mem, lp):
    k, v = jnp.split(rms(mem, lp['m_norm_g']) @ lp['w_mem_kv'], 2, axis=-1)
    return rms(heads(k, M_HEADS), lp['m_kn_g']), heads(v, M_HEADS)


def merge(x, outs, gates, lp):
    y = jnp.concatenate([o.reshape(o.shape[:2] + (-1,)) for o in outs], -1)
    y = rms(heads(y, N_OUT_HEADS), lp['out_g'].reshape(N_OUT_HEADS, HEAD_DIM))
    y = y.reshape(x.shape[:2] + (MIX_WIDTH,)) * jax.nn.silu(gates)
    return x + y @ lp['w_out']


def layer_prompt(x, mem, lp):
    S = x.shape[1]
    bi = branch_inputs(x, lp, jnp.arange(S))
    o_a = band_attn_prompt(bi['aq'], bi['ak'], bi['av'], lp['a_rel_bias'])
    kb, vb = mla_keys(bi['ckv'], bi['krope'], lp)
    o_b = chunk_causal_attn_blocks(bi['bq'], kb, vb, MLA_SCALE)
    o_c = stick_break_blocks(bi['cq'], bi['ck'], bi['cv'])
    mk, mv = mem_kv(mem, lp)
    o_m = dense_attn(bi['mq'], mk, mv, HEAD_DIM ** -0.5)
    y = merge(x, (o_a, o_b, o_c, o_m), bi['gates'], lp)
    keep = min(A_WIN, S)
    state = (bi['ak'][:, S - keep:], bi['av'][:, S - keep:], bi['ckv'], bi['krope'],
             bi['ck'], bi['cv'], mk, mv)
    return y, state


def layer_step(x, lp, ca_k, ca_v, cb_ckv, cb_kr, cc_k, cc_v, cm_k, cm_v):
    T = x.shape[1]
    n_past = cb_ckv.shape[1]
    pos = n_past + jnp.arange(T)
    bi = branch_inputs(x, lp, pos)
    ak_all = jnp.concatenate([ca_k, bi['ak']], 1)
    av_all = jnp.concatenate([ca_v, bi['av']], 1)
    o_a = band_attn_step(bi['aq'], ak_all, av_all, lp['a_rel_bias'])
    kb, vb = mla_keys(jnp.concatenate([cb_ckv, bi['ckv']], 1), jnp.concatenate([cb_kr, bi['krope']], 1), lp)
    o_b = dense_attn(bi['bq'], kb, vb, MLA_SCALE)
    o_c = stick_break(bi['cq'], jnp.concatenate([cc_k, bi['ck']], 1), jnp.concatenate([cc_v, bi['cv']], 1),
                      pos, jnp.arange(n_past + T))
    o_m = dense_attn(bi['mq'], cm_k, cm_v, HEAD_DIM ** -0.5)
    y = merge(x, (o_a, o_b, o_c, o_m), bi['gates'], lp)
    state = (ak_all[:, T:], av_all[:, T:], bi['ckv'], bi['krope'], bi['ck'], bi['cv'])
    return y, state


def setup_inputs(seed: int = 0) -> dict:
    key = jax.random.key(seed)
    ks = iter(jax.random.split(key, 40))

    def nrm(shape, scale):
        return jax.random.normal(next(ks), shape, jnp.float32) * scale

    def gain(shape):
        return 1.0 + nrm(shape, 0.02)

    n_a = min(A_WIN, PAST_LEN)
    return {
        'x_prompt': nrm((BATCH, SEQ, D_MODEL), 1.0),
        'x_sample': nrm((DEC_BATCH, DEC_SEQ, D_MODEL), 1.0),
        'mem_prompt': nrm((BATCH, N_MEM, D_MODEL), 1.0),
        'cache_a_k': nrm((DEPTH, DEC_BATCH, n_a, A_HEADS, HEAD_DIM), 1.0),
        'cache_a_v': nrm((DEPTH, DEC_BATCH, n_a, A_HEADS, HEAD_DIM), 1.0),
        'cache_b_ckv': nrm((DEPTH, DEC_BATCH, PAST_LEN, KV_LORA), 1.0),
        'cache_b_krope': nrm((DEPTH, DEC_BATCH, PAST_LEN, B_ROPE), 1.0),
        'cache_c_k': nrm((DEPTH, DEC_BATCH, PAST_LEN, C_HEADS, HEAD_DIM), 1.0),
        'cache_c_v': nrm((DEPTH, DEC_BATCH, PAST_LEN, C_HEADS, HEAD_DIM), 1.0),
        'cache_mem_k': nrm((DEPTH, DEC_BATCH, N_MEM, M_HEADS, HEAD_DIM), 1.0),
        'cache_mem_v': nrm((DEPTH, DEC_BATCH, N_MEM, M_HEADS, HEAD_DIM), 1.0),
        'norm_g': gain((DEPTH, D_MODEL)),
        'w_in': nrm((DEPTH, D_MODEL, IN_WIDTH), D_MODEL ** -0.5),
        'a_qn_g': gain((DEPTH, HEAD_DIM)),
        'a_kn_g': gain((DEPTH, HEAD_DIM)),
        'a_rel_bias': nrm((DEPTH, A_HEADS, 2 * REL_CLIP + 1), 0.1),
        'b_cq_g': gain((DEPTH, Q_LORA)),
        'b_wq_b': nrm((DEPTH, Q_LORA, B_HEADS * (B_NOPE + B_ROPE)), Q_LORA ** -0.5),
        'b_ckv_g': gain((DEPTH, KV_LORA)),
        'b_wkv_b': nrm((DEPTH, KV_LORA, B_HEADS * (B_NOPE + B_V)), KV_LORA ** -0.5),
        'b_qn_g': gain((DEPTH, B_NOPE)),
        'b_qr_g': gain((DEPTH, B_ROPE)),
        'b_kn_g': gain((DEPTH, B_NOPE)),
        'b_kr_g': gain((DEPTH, B_ROPE)),
        'm_norm_g': gain((DEPTH, D_MODEL)),
        'w_mem_kv': nrm((DEPTH, D_MODEL, 2 * M_HEADS * HEAD_DIM), D_MODEL ** -0.5),
        'm_qn_g': gain((DEPTH, HEAD_DIM)),
        'm_kn_g': gain((DEPTH, HEAD_DIM)),
        'out_g': gain((DEPTH, MIX_WIDTH)),
        'w_out': nrm((DEPTH, MIX_WIDTH, D_MODEL), 0.5 * MIX_WIDTH ** -0.5),
    }


def reference(x_prompt, x_sample, mem_prompt, cache_a_k, cache_a_v, cache_b_ckv, cache_b_krope,
              cache_c_k, cache_c_v, cache_mem_k, cache_mem_v, norm_g, w_in, a_qn_g, a_kn_g, a_rel_bias,
              b_cq_g, b_wq_b, b_ckv_g, b_wkv_b, b_qn_g, b_qr_g, b_kn_g, b_kr_g, m_norm_g, w_mem_kv,
              m_qn_g, m_kn_g, out_g, w_out):
    hp, hs = x_prompt, x_sample
    p_states, s_states = [], []
    for l in range(DEPTH):
        lp = dict(norm_g=norm_g[l], w_in=w_in[l], a_qn_g=a_qn_g[l], a_kn_g=a_kn_g[l],
                  a_rel_bias=a_rel_bias[l], b_cq_g=b_cq_g[l], b_wq_b=b_wq_b[l], b_ckv_g=b_ckv_g[l],
                  b_wkv_b=b_wkv_b[l], b_qn_g=b_qn_g[l], b_qr_g=b_qr_g[l], b_kn_g=b_kn_g[l],
                  b_kr_g=b_kr_g[l], m_norm_g=m_norm_g[l], w_mem_kv=w_mem_kv[l], m_qn_g=m_qn_g[l],
                  m_kn_g=m_kn_g[l], out_g=out_g[l], w_out=w_out[l])
        hp, sp = layer_prompt(hp, mem_prompt, lp)
        hs, ss = layer_step(hs, lp, cache_a_k[l], cache_a_v[l], cache_b_ckv[l], cache_b_krope[l],
                            cache_c_k[l], cache_c_v[l], cache_mem_k[l], cache_mem_v[l])
        p_states.append(sp)
        s_states.append(ss)
    a_k_p, a_v_p, ckv_p, kr_p, c_k_p, c_v_p, mem_k_p, mem_v_p = [
        jnp.stack([st[i] for st in p_states]) for i in range(8)]
    a_k_s, a_v_s, ckv_s, kr_s, c_k_s, c_v_s = [
        jnp.stack([st[i] for st in s_states]) for i in range(6)]
    return (hp, hs, a_k_p, a_v_p, ckv_p, kr_p, c_k_p, c_v_p, mem_k_p, mem_v_p,
            a_k_s, a_v_s, ckv_s, kr_s, c_k_s, c_v_s)
```

```python
import functools

import jax
import jax.numpy as jnp
from jax import lax
from jax.experimental import pallas as pl
from jax.experimental.pallas import tpu as pltpu

F32 = jnp.float32
BF16 = jnp.bfloat16

D_MODEL = 1024
HEAD_DIM = 64
HEADS = 4
GROUP = HEADS * HEAD_DIM
CHUNK = 64
A_PREV_CHUNKS = 8
A_WIN = A_PREV_CHUNKS * CHUNK
REL_CLIP = 256
B_NOPE = 64
B_ROPE = 32
B_QK = B_NOPE + B_ROPE
Q_LORA = 256
KV_LORA = 128
ROPE_THETA = 10000.0
MLA_SCALE = B_QK ** -0.5
HEAD_SCALE = HEAD_DIM ** -0.5
EPS = 1e-6
NEG = -1e30
LANES = 128
VMEM_LIMIT = 48 * 1024 * 1024

OFF_AQ, OFF_AK, OFF_AV, OFF_AG = 0, 256, 512, 768
OFF_BCQ, OFF_BCKV, OFF_BKR, OFF_BG = 1024, 1280, 1408, 1536
OFF_CQ, OFF_CK, OFF_CV, OFF_CG = 1792, 2048, 2304, 2560
OFF_MQ, OFF_MG = 2816, 3072
IN_WIDTH_PADDED = 3328
ROPE_LANE0 = B_NOPE


def _params(n_axes):
    return pltpu.CompilerParams(dimension_semantics=("arbitrary",) * n_axes, vmem_limit_bytes=VMEM_LIMIT)


def _split_bf16(x):
    hi = x.astype(BF16)
    lo = (x - hi.astype(F32)).astype(BF16)
    return hi, lo


def _dot(a, b):
    return jnp.dot(a, b, preferred_element_type=F32)


def _dot_nt(a, b):
    return lax.dot_general(a, b, (((1,), (1,)), ((), ())), preferred_element_type=F32)


def _group_mean(x2, g):
    hi, lo = _split_bf16(x2)
    return _dot(hi, g) + _dot(lo, g)


def _rope(x, c, s1, s2):
    return x * c + pltpu.roll(x, LANES - B_ROPE // 2, 1) * s1 + pltpu.roll(x, B_ROPE // 2, 1) * s2


def _proj_kernel(x_ref, c_ref, s1_ref, s2_ref, ng_ref, win_ref, g64_ref, aqg_ref, akg_ref, bcqg_ref, wq_ref,
                 gq_ref, gqg_ref, ckvg_ref, krg_ref, mqg_ref,
                 aq_o, ak_o, av_o, aks_o, avs_o, bq_o, ckv_o, kr_o, cq_o, ck_o, cv_o, cks_o, cvs_o, mq_o, gate_o,
                 scr):
    x = x_ref[0]
    ms = jnp.mean(x * x, axis=-1, keepdims=True)
    xn = (x * lax.rsqrt(ms + EPS) * ng_ref[...]).astype(BF16)
    c, s1, s2 = c_ref[...], s1_ref[...], s2_ref[...]

    def seg(off, width):
        return _dot(xn, win_ref[:, off:off + width])

    def head_rms(h, gain_ref):
        m = _group_mean(h * h, g64_ref[...])
        return h * lax.rsqrt(m + EPS) * gain_ref[...]

    def put_heads(o_ref, val, base=0):
        scr[...] = val
        for h in range(HEADS):
            o_ref[0, base + h] = scr[:, h * HEAD_DIM:(h + 1) * HEAD_DIM].astype(o_ref.dtype)

    put_heads(aq_o, head_rms(seg(OFF_AQ, GROUP), aqg_ref) * HEAD_SCALE)
    ak = head_rms(seg(OFF_AK, GROUP), akg_ref)
    aks_o[0] = ak
    put_heads(ak_o, ak)
    av = seg(OFF_AV, GROUP)
    avs_o[0] = av
    put_heads(av_o, av)

    for gi, off in enumerate((OFF_AG, OFF_BG, OFF_CG, OFF_MG)):
        g = seg(off, GROUP)
        put_heads(gate_o, g * jax.nn.sigmoid(g), base=HEADS * gi)

    h = seg(OFF_BCQ, Q_LORA)
    cq = (h * lax.rsqrt(jnp.mean(h * h, axis=-1, keepdims=True) + EPS) * bcqg_ref[...]).astype(BF16)
    qb = _dot(cq, wq_ref[...])
    for half in range(2):
        part = qb[:, half * 2 * LANES:(half + 1) * 2 * LANES]
        m = _group_mean(part * part, gq_ref[...])
        normed = part * lax.rsqrt(m + EPS) * gqg_ref[:, half * 2 * LANES:(half + 1) * 2 * LANES]
        for hh in range(2):
            xh = normed[:, hh * LANES:(hh + 1) * LANES]
            bq_o[0, 2 * half + hh] = (_rope(xh, c, s1, s2) * MLA_SCALE).astype(BF16)

    h = seg(OFF_BCKV, KV_LORA)
    ckv_o[0] = h * lax.rsqrt(jnp.mean(h * h, axis=-1, keepdims=True) + EPS) * ckvg_ref[...]
    h = seg(OFF_BKR, LANES)
    ms = jnp.sum(h * h, axis=-1, keepdims=True) * (1.0 / B_ROPE)
    kr = _rope(h * lax.rsqrt(ms + EPS) * krg_ref[...], c, s1, s2)
    scr[:, 0:LANES] = kr
    kr_o[0] = scr[:, ROPE_LANE0:ROPE_LANE0 + B_ROPE]

    put_heads(cq_o, seg(OFF_CQ, GROUP) * HEAD_SCALE)
    ck = seg(OFF_CK, GROUP)
    cks_o[0] = ck
    put_heads(ck_o, ck)
    cv = seg(OFF_CV, GROUP)
    cvs_o[0] = cv
    put_heads(cv_o, cv)

    put_heads(mq_o, head_rms(seg(OFF_MQ, GROUP), mqg_ref) * HEAD_SCALE)


def _proj(x, rope_tabs, lw, tm):
    n, t, _ = x.shape
    nt = t // tm
    c, s1, s2 = rope_tabs

    def full(a):
        return pl.BlockSpec(a.shape, lambda i, j: (0,) * a.ndim)

    tab_spec = pl.BlockSpec((tm, LANES), lambda i, j: (j, 0))
    weights = (lw['norm_g'], lw['w_in'], lw['g64'], lw['a_qn_g'], lw['a_kn_g'], lw['b_cq_g'], lw['wq'], lw['gq'],
               lw['gq_gain'], lw['b_ckv_g'], lw['kr_gain'], lw['m_qn_g'])

    def hm(width, dtype, heads=HEADS):
        return (jax.ShapeDtypeStruct((n, heads, t, width), dtype),
                pl.BlockSpec((1, heads, tm, width), lambda i, j: (i, 0, j, 0)))

    def tmaj(width):
        return (jax.ShapeDtypeStruct((n, t, width), F32), pl.BlockSpec((1, tm, width), lambda i, j: (i, j, 0)))

    outs = [hm(HEAD_DIM, BF16), hm(HEAD_DIM, BF16), hm(HEAD_DIM, BF16), tmaj(GROUP), tmaj(GROUP),
            hm(LANES, BF16), tmaj(KV_LORA), tmaj(B_ROPE),
            hm(HEAD_DIM, BF16), hm(HEAD_DIM, BF16), hm(HEAD_DIM, BF16), tmaj(GROUP), tmaj(GROUP),
            hm(HEAD_DIM, BF16), hm(HEAD_DIM, F32, 4 * HEADS)]
    names = ('aq', 'ak', 'av', 'ak_s', 'av_s', 'bq', 'ckv', 'kr', 'cq', 'ck', 'cv', 'ck_s', 'cv_s', 'mq', 'gates')
    res = pl.pallas_call(
        _proj_kernel,
        grid=(n, nt),
        in_specs=[pl.BlockSpec((1, tm, D_MODEL), lambda i, j: (i, j, 0)), tab_spec, tab_spec, tab_spec]
        + [full(w) for w in weights],
        out_specs=[o[1] for o in outs],
        out_shape=[o[0] for o in outs],
        scratch_shapes=[pltpu.VMEM((tm, GROUP), F32)],
        compiler_params=_params(2),
        name='proj',
    )(x, c, s1, s2, *weights)
    return dict(zip(names, res))


def _kv_kernel(ckv_ref, kr_ref, wkv_ref, place_ref, gk_ref, gkg_ref, k_o, v_o, scr):
    kv = _dot(ckv_ref[0].astype(BF16), wkv_ref[...])
    kr = _dot(kr_ref[0].astype(BF16), place_ref[...])
    for half in range(2):
        part = kv[:, half * 2 * LANES:(half + 1) * 2 * LANES]
        m = _group_mean(part * part, gk_ref[...])
        normed = part * lax.rsqrt(m + EPS) * gkg_ref[:, half * 2 * LANES:(half + 1) * 2 * LANES]
        for hh in range(2):
            k_o[0, 2 * half + hh] = (normed[:, hh * LANES:(hh + 1) * LANES] + kr).astype(BF16)
    scr[...] = kv[:, HEADS * LANES:]
    for h in range(HEADS):
        v_o[0, h] = scr[:, h * HEAD_DIM:(h + 1) * HEAD_DIM].astype(BF16)


def _kv_expand(ckv, kr, lw, tl):
    n, l, _ = ckv.shape
    weights = (lw['wkv'], lw['place'], lw['gk'], lw['gk_gain'])
    return pl.pallas_call(
        _kv_kernel,
        grid=(n, l // tl),
        in_specs=[pl.BlockSpec((1, tl, KV_LORA), lambda i, j: (i, j, 0)),
                  pl.BlockSpec((1, tl, B_ROPE), lambda i, j: (i, j, 0))]
        + [pl.BlockSpec(w.shape, lambda i, j: (0, 0)) for w in weights],
        out_specs=[pl.BlockSpec((1, HEADS, tl, LANES), lambda i, j: (i, 0, j, 0)),
                   pl.BlockSpec((1, HEADS, tl, HEAD_DIM), lambda i, j: (i, 0, j, 0))],
        out_shape=[jax.ShapeDtypeStruct((n, HEADS, l, LANES), BF16),
                   jax.ShapeDtypeStruct((n, HEADS, l, HEAD_DIM), BF16)],
        scratch_shapes=[pltpu.VMEM((tl, GROUP), F32)],
        compiler_params=_params(2),
        name='kv_expand',
    )(ckv, kr, *weights)


def _memkv_kernel(mem_ref, ng_ref, w_ref, g64_ref, kg_ref, k_o, v_o):
    x = mem_ref[0]
    xn = (x * lax.rsqrt(jnp.mean(x * x, axis=-1, keepdims=True) + EPS) * ng_ref[...]).astype(BF16)
    kv = _dot(xn, w_ref[...])
    k = kv[:, :GROUP]
    m = _group_mean(k * k, g64_ref[...])
    k_o[0] = k * lax.rsqrt(m + EPS) * kg_ref[...]
    v_o[0] = kv[:, GROUP:]


def _mem_kv(mem, lw):
    n, nm, _ = mem.shape
    weights = (lw['m_norm_g'], lw['w_mem_kv'], lw['g64'], lw['m_kn_g'])
    return pl.pallas_call(
        _memkv_kernel,
        grid=(n,),
        in_specs=[pl.BlockSpec((1, nm, D_MODEL), lambda i: (i, 0, 0))]
        + [pl.BlockSpec(w.shape, lambda i: (0, 0)) for w in weights],
        out_specs=[pl.BlockSpec((1, nm, GROUP), lambda i: (i, 0, 0))] * 2,
        out_shape=[jax.ShapeDtypeStruct((n, nm, GROUP), F32)] * 2,
        compiler_params=_params(1),
        name='mem_kv',
    )(mem, *weights)


def _band_kernel(q_ref, k_ref, v_ref, bias_ref, o_ref, *, rows, span, mask_front):
    start = pl.multiple_of(pl.program_id(2) * rows, CHUNK)
    k = k_ref[0, 0, pl.ds(start, span), :]
    v = v_ref[0, 0, pl.ds(start, span), :]
    s = _dot_nt(q_ref[0, 0], k) + bias_ref[0]
    if mask_front:
        kpos = start - A_WIN + lax.broadcasted_iota(jnp.int32, (rows, span), 1)
        s = jnp.where(kpos >= 0, s, NEG)
    p = jnp.exp(s - jnp.max(s, axis=-1, keepdims=True))
    l = jnp.sum(p, axis=-1, keepdims=True)
    o_ref[0, 0] = _dot(p.astype(BF16), v) / l


def _band_attn(q, k, v, bias, g, mask_front):
    n, _, t, _ = q.shape
    rows, span = CHUNK * g, CHUNK * (g + A_PREV_CHUNKS)
    lk = k.shape[2]
    return pl.pallas_call(
        functools.partial(_band_kernel, rows=rows, span=span, mask_front=mask_front),
        grid=(n, HEADS, t // rows),
        in_specs=[pl.BlockSpec((1, 1, rows, HEAD_DIM), lambda i, h, j: (i, h, j, 0)),
                  pl.BlockSpec((1, 1, lk, HEAD_DIM), lambda i, h, j: (i, h, 0, 0)),
                  pl.BlockSpec((1, 1, lk, HEAD_DIM), lambda i, h, j: (i, h, 0, 0)),
                  pl.BlockSpec((1, rows, span), lambda i, h, j: (h, 0, 0))],
        out_specs=pl.BlockSpec((1, 1, rows, HEAD_DIM), lambda i, h, j: (i, h, j, 0)),
        out_shape=jax.ShapeDtypeStruct((n, HEADS, t, HEAD_DIM), F32),
        compiler_params=_params(3),
        name='band_attn',
    )(q, k, v, bias)


def _band_bias(tab, g):
    rows, span = CHUNK * g, CHUNK * (g + A_PREV_CHUNKS)
    i = jnp.arange(rows)[:, None]
    j = jnp.arange(span)[None, :]
    rel = i + A_WIN - j
    lo = (i // CHUNK) * CHUNK
    inband = (j >= lo) & (j < lo + CHUNK * (A_PREV_CHUNKS + 1))
    b = tab[:, jnp.clip(rel, -REL_CLIP, REL_CLIP) + REL_CLIP].astype(F32)
    return jnp.where(inband[None], b, NEG)


def _softmax_kernel(q_ref, k_ref, v_ref, o_ref, *, tq, tk, q0, causal, n_kblocks):
    q = q_ref[0, 0]
    q_start = q0 + pl.program_id(2) * tq

    def step(kb, carry, masked):
        m, l, acc = carry
        off = pl.multiple_of(kb * tk, tk)
        k = k_ref[0, 0, pl.ds(off, tk), :]
        v = v_ref[0, 0, pl.ds(off, tk), :]
        s = _dot_nt(q, k)
        if masked:
            qchunk = (q_start + lax.broadcasted_iota(jnp.int32, (tq, tk), 0)) // CHUNK
            kchunk = (off + lax.broadcasted_iota(jnp.int32, (tq, tk), 1)) // CHUNK
            s = jnp.where(kchunk <= qchunk, s, NEG)
        m_new = jnp.maximum(m, jnp.max(s, axis=-1, keepdims=True))
        alpha = jnp.exp(m - m_new)
        p = jnp.exp(s - m_new)
        l = alpha * l + jnp.sum(p, axis=-1, keepdims=True)
        acc = alpha * acc + _dot(p.astype(BF16), v)
        return m_new, l, acc

    init = (jnp.full((tq, 1), NEG, F32), jnp.zeros((tq, 1), F32), jnp.zeros((tq, HEAD_DIM), F32))
    if causal:
        n_full = (q_start + CHUNK) // tk
        n_all = (q_start + tq + tk - 1) // tk
        carry = lax.fori_loop(0, n_full, functools.partial(step, masked=False), init)
        carry = lax.fori_loop(n_full, n_all, functools.partial(step, masked=True), carry)
    else:
        carry = lax.fori_loop(0, n_kblocks, functools.partial(step, masked=False), init)
    _, l, acc = carry
    o_ref[0, 0] = acc / l


def _softmax_attn(q, k, v, *, tq, tk, q0=0, causal):
    n, _, t, dk = q.shape
    lk = k.shape[2]
    assert t % tq == 0 and lk % tk == 0 and tq % CHUNK == 0 and q0 % CHUNK == 0
    assert not causal or q0 + t <= lk
    return pl.pallas_call(
        functools.partial(_softmax_kernel, tq=tq, tk=tk, q0=q0, causal=causal, n_kblocks=lk // tk),
        grid=(n, HEADS, t // tq),
        in_specs=[pl.BlockSpec((1, 1, tq, dk), lambda i, h, j: (i, h, j, 0)),
                  pl.BlockSpec((1, 1, lk, dk), lambda i, h, j: (i, h, 0, 0)),
                  pl.BlockSpec((1, 1, lk, HEAD_DIM), lambda i, h, j: (i, h, 0, 0))],
        out_specs=pl.BlockSpec((1, 1, tq, HEAD_DIM), lambda i, h, j: (i, h, j, 0)),
        out_shape=jax.ShapeDtypeStruct((n, HEADS, t, HEAD_DIM), F32),
        compiler_params=_params(3),
        name='softmax_attn',
    )(q, k, v)


def _stick_kernel(q_ref, k_ref, v_ref, tri_ref, o_ref, *, tq, tk, q0):
    q = q_ref[0, 0]
    q_start = q0 + pl.program_id(2) * tq
    tri = tri_ref[...]

    def step(i, carry, hi_block, masked):
        run, acc = carry
        kb = hi_block - 1 - i
        off = pl.multiple_of(kb * tk, tk)
        k = k_ref[0, 0, pl.ds(off, tk), :]
        v = v_ref[0, 0, pl.ds(off, tk), :]
        z = _dot_nt(q, k)
        log_1m = jnp.minimum(-z, 0.0) - jnp.log1p(jnp.exp(-jnp.abs(z)))
        if masked:
            qpos = q_start + lax.broadcasted_iota(jnp.int32, (tq, tk), 0)
            kpos = off + lax.broadcasted_iota(jnp.int32, (tq, tk), 1)
            mask = kpos < qpos
            log_1m = jnp.where(mask, log_1m, 0.0)
        hi, lo = _split_bf16(log_1m)
        tail = _dot(hi, tri) + _dot(lo, tri) + run
        w = jnp.exp(z + log_1m + tail)
        if masked:
            w = jnp.where(mask, w, 0.0)
        acc = acc + _dot(w.astype(BF16), v)
        run = run + jnp.sum(log_1m, axis=-1, keepdims=True)
        return run, acc

    n_full = q_start // tk
    n_all = (q_start + tq + tk - 1) // tk
    carry = (jnp.zeros((tq, 1), F32), jnp.zeros((tq, HEAD_DIM), F32))
    carry = lax.fori_loop(0, n_all - n_full, functools.partial(step, hi_block=n_all, masked=True), carry)
    carry = lax.fori_loop(0, n_full, functools.partial(step, hi_block=n_full, masked=False), carry)
    o_ref[0, 0] = carry[1]


def _stick_attn(q, k, v, *, tq, tk, q0=0):
    n, _, t, _ = q.shape
    lk = k.shape[2]
    assert t % tq == 0 and lk % tk == 0 and q0 + t <= lk
    r = jnp.arange(tk)
    tri = (r[:, None] > r[None, :]).astype(BF16)
    return pl.pallas_call(
        functools.partial(_stick_kernel, tq=tq, tk=tk, q0=q0),
        grid=(n, HEADS, t // tq),
        in_specs=[pl.BlockSpec((1, 1, tq, HEAD_DIM), lambda i, h, j: (i, h, j, 0)),
                  pl.BlockSpec((1, 1, lk, HEAD_DIM), lambda i, h, j: (i, h, 0, 0)),
                  pl.BlockSpec((1, 1, lk, HEAD_DIM), lambda i, h, j: (i, h, 0, 0)),
                  pl.BlockSpec((tk, tk), lambda i, h, j: (0, 0))],
        out_specs=pl.BlockSpec((1, 1, tq, HEAD_DIM), lambda i, h, j: (i, h, j, 0)),
        out_shape=jax.ShapeDtypeStruct((n, HEADS, t, HEAD_DIM), F32),
        compiler_params=_params(3),
        name='stick_attn',
    )(q, k, v, tri)


def _merge_kernel(x_ref, oa_ref, ob_ref, oc_ref, om_ref, gate_ref, og_ref, w_ref, y_ref):
    acc = x_ref[0]
    for gi, o_ref in enumerate((oa_ref, ob_ref, oc_ref, om_ref)):
        for h in range(HEADS):
            j = HEADS * gi + h
            o = o_ref[0, h]
            y = o * lax.rsqrt(jnp.mean(o * o, axis=-1, keepdims=True) + EPS) * og_ref[j] * gate_ref[0, j]
            acc = acc + _dot(y.astype(BF16), w_ref[j])
    y_ref[0] = acc


def _merge(x, o_a, o_b, o_c, o_m, gates, lw, tm):
    n, t, _ = x.shape
    hspec = pl.BlockSpec((1, HEADS, tm, HEAD_DIM), lambda i, j: (i, 0, j, 0))
    return pl.pallas_call(
        _merge_kernel,
        grid=(n, t // tm),
        in_specs=[pl.BlockSpec((1, tm, D_MODEL), lambda i, j: (i, j, 0)), hspec, hspec, hspec, hspec,
                  pl.BlockSpec((1, 4 * HEADS, tm, HEAD_DIM), lambda i, j: (i, 0, j, 0)),
                  pl.BlockSpec((4 * HEADS, 1, HEAD_DIM), lambda i, j: (0, 0, 0)),
                  pl.BlockSpec((4 * HEADS, HEAD_DIM, D_MODEL), lambda i, j: (0, 0, 0))],
        out_specs=pl.BlockSpec((1, tm, D_MODEL), lambda i, j: (i, j, 0)),
        out_shape=jax.ShapeDtypeStruct((n, t, D_MODEL), F32),
        compiler_params=_params(2),
        name='merge',
    )(x, o_a, o_b, o_c, o_m, gates, lw['out_g'], lw['w_out'])


def _block_diag(block, reps):
    return jnp.kron(jnp.eye(reps, dtype=F32), block)


def _layer_weights(l, p):
    w_in = p['w_in'][l]
    zeros = lambda c: jnp.zeros((D_MODEL, c), F32)
    kr0 = 1408
    w_in = jnp.concatenate([w_in[:, :kr0], zeros(ROPE_LANE0), w_in[:, kr0:kr0 + B_ROPE],
                            zeros(LANES - ROPE_LANE0 - B_ROPE), w_in[:, kr0 + B_ROPE:]], axis=1)
    assert w_in.shape[1] == IN_WIDTH_PADDED
    ones = lambda k: jnp.full((k, k), 1.0 / k, F32)
    zpad = lambda k: jnp.zeros((k, k), F32)
    tile = lambda g, reps=HEADS: jnp.tile(g, reps)[None, :]
    pad_to = lambda g, k: jnp.concatenate([g, jnp.zeros((k - g.shape[0],), F32)])
    wq = jnp.pad(p['b_wq_b'][l].reshape(Q_LORA, HEADS, B_QK), ((0, 0), (0, 0), (0, LANES - B_QK)))
    wkv = p['b_wkv_b'][l].reshape(KV_LORA, HEADS, B_NOPE + HEAD_DIM)
    wk = jnp.pad(wkv[:, :, :B_NOPE], ((0, 0), (0, 0), (0, LANES - B_NOPE))).reshape(KV_LORA, HEADS * LANES)
    wv = wkv[:, :, B_NOPE:].reshape(KV_LORA, GROUP)
    gq128 = jax.scipy.linalg.block_diag(ones(B_NOPE), ones(B_ROPE), zpad(LANES - B_QK))
    gk128 = jax.scipy.linalg.block_diag(ones(B_NOPE), zpad(LANES - B_NOPE))
    place = jnp.zeros((B_ROPE, LANES), F32).at[jnp.arange(B_ROPE), ROPE_LANE0 + jnp.arange(B_ROPE)].set(1.0)
    return dict(
        norm_g=p['norm_g'][l][None, :],
        w_in=w_in.astype(BF16),
        g64=_block_diag(ones(HEAD_DIM), HEADS).astype(BF16),
        a_qn_g=tile(p['a_qn_g'][l]), a_kn_g=tile(p['a_kn_g'][l]),
        b_cq_g=p['b_cq_g'][l][None, :],
        wq=wq.reshape(Q_LORA, HEADS * LANES).astype(BF16),
        gq=_block_diag(gq128, 2).astype(BF16),
        gq_gain=tile(jnp.concatenate([p['b_qn_g'][l], pad_to(p['b_qr_g'][l], LANES - B_NOPE)])),
        b_ckv_g=p['b_ckv_g'][l][None, :],
        kr_gain=jnp.concatenate([jnp.zeros((ROPE_LANE0,), F32), pad_to(p['b_kr_g'][l], LANES - ROPE_LANE0)])[None, :],
        m_qn_g=tile(p['m_qn_g'][l]),
        wkv=jnp.concatenate([wk, wv], axis=1).astype(BF16),
        place=place.astype(BF16),
        gk=_block_diag(gk128, 2).astype(BF16),
        gk_gain=tile(pad_to(p['b_kn_g'][l], LANES)),
        m_norm_g=p['m_norm_g'][l][None, :],
        w_mem_kv=p['w_mem_kv'][l].astype(BF16),
        m_kn_g=tile(p['m_kn_g'][l]),
        out_g=p['out_g'][l].reshape(4 * HEADS, 1, HEAD_DIM),
        w_out=p['w_out'][l].reshape(4 * HEADS, HEAD_DIM, D_MODEL).astype(BF16),
        a_rel_bias=p['a_rel_bias'][l],
    )


def _rope_tables(pos):
    half = B_ROPE // 2
    freqs = ROPE_THETA ** (-jnp.arange(half, dtype=F32) / half)
    ang = pos.astype(F32)[:, None] * freqs[None, :]
    cos, sin = jnp.cos(ang), jnp.sin(ang)
    t = pos.shape[0]
    one, zero = jnp.ones((t, ROPE_LANE0), F32), jnp.zeros((t, ROPE_LANE0), F32)
    tail, z16 = jnp.zeros((t, LANES - ROPE_LANE0 - B_ROPE), F32), jnp.zeros((t, half), F32)
    c = jnp.concatenate([one, cos, cos, tail], axis=1)
    s1 = jnp.concatenate([zero, -sin, z16, tail], axis=1)
    s2 = jnp.concatenate([zero, z16, sin, tail], axis=1)
    return c, s1, s2


def _head_major(a):
    return a.transpose(0, 2, 1, 3)


def _token_major(a):
    return a.reshape(a.shape[0], a.shape[1], HEADS, HEAD_DIM)


def _pad_rows(a, front, back):
    return jnp.pad(a, ((0, 0), (0, 0), (front, back), (0, 0)))


def _tiles(s):
    tm = min(256, s)
    tq = min(512, s)
    tk = min(256, s)
    g = min(4, s // CHUNK)
    return tm, tq, tk, g


def _layer_prompt(x, mem, lw, rope_tabs):
    n, s, _ = x.shape
    tm, tq, tk, g = _tiles(s)
    pr = _proj(x, rope_tabs, lw, tm)
    kb, vb = _kv_expand(pr['ckv'], pr['kr'], lw, tm)
    bias = _band_bias(lw['a_rel_bias'], g)
    o_a = _band_attn(pr['aq'], _pad_rows(pr['ak'], A_WIN, 0), _pad_rows(pr['av'], A_WIN, 0), bias, g, True)
    o_b = _softmax_attn(pr['bq'], kb, vb, tq=tq, tk=tk, causal=True)
    o_c = _stick_attn(pr['cq'], pr['ck'], pr['cv'], tq=tq, tk=tk)
    mk, mv = _mem_kv(mem, lw)
    mk4, mv4 = _token_major(mk), _token_major(mv)
    o_m = _softmax_attn(pr['mq'], _head_major(mk4).astype(BF16), _head_major(mv4).astype(BF16),
                        tq=tq, tk=mk.shape[1], causal=False)
    y = _merge(x, o_a, o_b, o_c, o_m, pr['gates'], lw, tm)
    keep = min(A_WIN, s)
    state = (_token_major(pr['ak_s'][:, s - keep:]), _token_major(pr['av_s'][:, s - keep:]), pr['ckv'], pr['kr'],
             _token_major(pr['ck_s']), _token_major(pr['cv_s']), mk4, mv4)
    return y, state


def _layer_step(x, lw, rope_tabs, ca_k, ca_v, cb_ckv, cb_kr, cc_k, cc_v, cm_k, cm_v):
    n, t, _ = x.shape
    n_past = cb_ckv.shape[1]
    assert t == CHUNK and n_past % CHUNK == 0 and ca_k.shape[1] == A_WIN
    tk = 128
    lp = -(-(n_past + t) // tk) * tk
    back = lp - n_past - t
    pr = _proj(x, rope_tabs, lw, t)
    ckv_all = jnp.pad(jnp.concatenate([cb_ckv, pr['ckv']], 1), ((0, 0), (0, back), (0, 0)))
    kr_all = jnp.pad(jnp.concatenate([cb_kr, pr['kr']], 1), ((0, 0), (0, back), (0, 0)))
    kb, vb = _kv_expand(ckv_all, kr_all, lw, tk)
    bias = _band_bias(lw['a_rel_bias'], 1)
    ak_all = jnp.concatenate([_head_major(ca_k).astype(BF16), pr['ak']], 2)
    av_all = jnp.concatenate([_head_major(ca_v).astype(BF16), pr['av']], 2)
    o_a = _band_attn(pr['aq'], ak_all, av_all, bias, 1, False)
    o_b = _softmax_attn(pr['bq'], kb, vb, tq=t, tk=tk, q0=n_past, causal=True)
    ck_all = _pad_rows(jnp.concatenate([_head_major(cc_k).astype(BF16), pr['ck']], 2), 0, back)
    cv_all = _pad_rows(jnp.concatenate([_head_major(cc_v).astype(BF16), pr['cv']], 2), 0, back)
    o_c = _stick_attn(pr['cq'], ck_all, cv_all, tq=t, tk=tk, q0=n_past)
    o_m = _softmax_attn(pr['mq'], _head_major(cm_k).astype(BF16), _head_major(cm_v).astype(BF16),
                        tq=t, tk=cm_k.shape[1], causal=False)
    y = _merge(x, o_a, o_b, o_c, o_m, pr['gates'], lw, t)
    state = (jnp.concatenate([ca_k[:, t:], _token_major(pr['ak_s'])], 1),
             jnp.concatenate([ca_v[:, t:], _token_major(pr['av_s'])], 1),
             pr['ckv'], pr['kr'], _token_major(pr['ck_s']), _token_major(pr['cv_s']))
    return y, state


def kernel(x_prompt, x_sample, mem_prompt, cache_a_k, cache_a_v, cache_b_ckv, cache_b_krope, cache_c_k, cache_c_v,
           cache_mem_k, cache_mem_v, norm_g, w_in, a_qn_g, a_kn_g, a_rel_bias, b_cq_g, b_wq_b, b_ckv_g, b_wkv_b,
           b_qn_g, b_qr_g, b_kn_g, b_kr_g, m_norm_g, w_mem_kv, m_qn_g, m_kn_g, out_g, w_out):
    p = dict(norm_g=norm_g, w_in=w_in, a_qn_g=a_qn_g, a_kn_g=a_kn_g, a_rel_bias=a_rel_bias, b_cq_g=b_cq_g,
             b_wq_b=b_wq_b, b_ckv_g=b_ckv_g, b_wkv_b=b_wkv_b, b_qn_g=b_qn_g, b_qr_g=b_qr_g, b_kn_g=b_kn_g,
             b_kr_g=b_kr_g, m_norm_g=m_norm_g, w_mem_kv=w_mem_kv, m_qn_g=m_qn_g, m_kn_g=m_kn_g, out_g=out_g,
             w_out=w_out)
    depth = w_in.shape[0]
    s, t, n_past = x_prompt.shape[1], x_sample.shape[1], cache_b_ckv.shape[2]
    tabs_p = _rope_tables(jnp.arange(s))
    tabs_s = _rope_tables(n_past + jnp.arange(t))
    hp, hs = x_prompt, x_sample
    p_states, s_states = [], []
    for l in range(depth):
        lw = _layer_weights(l, p)
        hp, sp = _layer_prompt(hp, mem_prompt, lw, tabs_p)
        hs, ss = _layer_step(hs, lw, tabs_s, cache_a_k[l], cache_a_v[l], cache_b_ckv[l], cache_b_krope[l],
                             cache_c_k[l], cache_c_v[l], cache_mem_k[l], cache_mem_v[l])
        p_states.append(sp)
        s_states.append(ss)
    outs_p = [jnp.stack([st[i] for st in p_states]) for i in range(8)]
    outs_s = [jnp.stack([st[i] for st in s_states]) for i in range(6)]
    return (hp, hs, *outs_p, *outs_s)
```

```python
import functools

import jax
import jax.numpy as jnp
from jax import lax
from jax.experimental import pallas as pl
from jax.experimental.pallas import tpu as pltpu

F32 = jnp.float32
BF16 = jnp.bfloat16

D_MODEL = 1024
HEAD_DIM = 64
HEADS = 4
GROUP = HEADS * HEAD_DIM
CHUNK = 64
A_PREV_CHUNKS = 8
A_WIN = A_PREV_CHUNKS * CHUNK
REL_CLIP = 256
B_NOPE = 64
B_ROPE = 32
B_QK = B_NOPE + B_ROPE
Q_LORA = 256
KV_LORA = 128
ROPE_THETA = 10000.0
MLA_SCALE = B_QK ** -0.5
HEAD_SCALE = HEAD_DIM ** -0.5
EPS = 1e-6
NEG = -1e30
LOG2E = 1.4426950408889634
SUB = 256
LANES = 128
VMEM_LIMIT = 48 * 1024 * 1024

OFF_AQ, OFF_AK, OFF_AV, OFF_AG = 0, 256, 512, 768
OFF_BCQ, OFF_BCKV, OFF_BKR, OFF_BG = 1024, 1280, 1408, 1536
OFF_CQ, OFF_CK, OFF_CV, OFF_CG = 1792, 2048, 2304, 2560
OFF_MQ, OFF_MG = 2816, 3072
IN_WIDTH_PADDED = 3328
ROPE_LANE0 = B_NOPE


def _params(n_axes):
    return pltpu.CompilerParams(dimension_semantics=("arbitrary",) * n_axes, vmem_limit_bytes=VMEM_LIMIT)


def _split_bf16(x):
    hi = x.astype(BF16)
    lo = (x - hi.astype(F32)).astype(BF16)
    return hi, lo


def _dot(a, b):
    return jnp.dot(a, b, preferred_element_type=F32)


def _dot_nt(a, b):
    return lax.dot_general(a, b, (((1,), (1,)), ((), ())), preferred_element_type=F32)


def _group_mean(x2, g):
    hi, lo = _split_bf16(x2)
    return _dot(hi, g) + _dot(lo, g)


def _rope(x, c, s1, s2):
    return x * c + pltpu.roll(x, LANES - B_ROPE // 2, 1) * s1 + pltpu.roll(x, B_ROPE // 2, 1) * s2


def _proj_kernel(x_ref, c_ref, s1_ref, s2_ref, ng_ref, win_ref, g64_ref, aqg_ref, akg_ref, bcqg_ref, wq_ref,
                 gq_ref, gqg_ref, ckvg_ref, krg_ref, mqg_ref,
                 aq_o, ak_o, av_o, aks_o, avs_o, bq_o, ckv_o, kr_o, cq_o, ck_o, cv_o, cks_o, cvs_o, mq_o, gate_o,
                 scr):
    x = x_ref[0]
    ms = jnp.mean(x * x, axis=-1, keepdims=True)
    xn = (x * lax.rsqrt(ms + EPS) * ng_ref[...]).astype(BF16)
    c, s1, s2 = c_ref[...], s1_ref[...], s2_ref[...]

    def seg(off, width):
        return _dot(xn, win_ref[:, off:off + width])

    def head_rms(h, gain_ref):
        m = _group_mean(h * h, g64_ref[...])
        return h * lax.rsqrt(m + EPS) * gain_ref[...]

    def put_heads(o_ref, val, base=0):
        scr[...] = val
        for h in range(HEADS):
            o_ref[0, base + h] = scr[:, h * HEAD_DIM:(h + 1) * HEAD_DIM].astype(o_ref.dtype)

    put_heads(aq_o, head_rms(seg(OFF_AQ, GROUP), aqg_ref) * HEAD_SCALE)
    ak = head_rms(seg(OFF_AK, GROUP), akg_ref)
    aks_o[0] = ak
    put_heads(ak_o, ak)
    av = seg(OFF_AV, GROUP)
    avs_o[0] = av
    put_heads(av_o, av)

    for gi, off in enumerate((OFF_AG, OFF_BG, OFF_CG, OFF_MG)):
        g = seg(off, GROUP)
        put_heads(gate_o, g * jax.nn.sigmoid(g), base=HEADS * gi)

    h = seg(OFF_BCQ, Q_LORA)
    cq = (h * lax.rsqrt(jnp.mean(h * h, axis=-1, keepdims=True) + EPS) * bcqg_ref[...]).astype(BF16)
    qb = _dot(cq, wq_ref[...])
    for half in range(2):
        part = qb[:, half * 2 * LANES:(half + 1) * 2 * LANES]
        m = _group_mean(part * part, gq_ref[...])
        normed = part * lax.rsqrt(m + EPS) * gqg_ref[:, half * 2 * LANES:(half + 1) * 2 * LANES]
        for hh in range(2):
            xh = normed[:, hh * LANES:(hh + 1) * LANES]
            bq_o[0, 2 * half + hh] = (_rope(xh, c, s1, s2) * (MLA_SCALE * LOG2E)).astype(BF16)

    h = seg(OFF_BCKV, KV_LORA)
    ckv_o[0] = h * lax.rsqrt(jnp.mean(h * h, axis=-1, keepdims=True) + EPS) * ckvg_ref[...]
    h = seg(OFF_BKR, LANES)
    ms = jnp.sum(h * h, axis=-1, keepdims=True) * (1.0 / B_ROPE)
    kr = _rope(h * lax.rsqrt(ms + EPS) * krg_ref[...], c, s1, s2)
    scr[:, 0:LANES] = kr
    kr_o[0] = scr[:, ROPE_LANE0:ROPE_LANE0 + B_ROPE]

    put_heads(cq_o, seg(OFF_CQ, GROUP) * (HEAD_SCALE * LOG2E))
    ck = seg(OFF_CK, GROUP)
    cks_o[0] = ck
    put_heads(ck_o, ck)
    cv = seg(OFF_CV, GROUP)
    cvs_o[0] = cv
    put_heads(cv_o, cv)

    put_heads(mq_o, head_rms(seg(OFF_MQ, GROUP), mqg_ref) * (HEAD_SCALE * LOG2E))


def _proj(x, rope_tabs, lw, tm):
    n, t, _ = x.shape
    nt = t // tm
    c, s1, s2 = rope_tabs

    def full(a):
        return pl.BlockSpec(a.shape, lambda i, j: (0,) * a.ndim)

    tab_spec = pl.BlockSpec((tm, LANES), lambda i, j: (j, 0))
    weights = (lw['norm_g'], lw['w_in'], lw['g64'], lw['a_qn_g'], lw['a_kn_g'], lw['b_cq_g'], lw['wq'], lw['gq'],
               lw['gq_gain'], lw['b_ckv_g'], lw['kr_gain'], lw['m_qn_g'])

    def hm(width, dtype, heads=HEADS):
        return (jax.ShapeDtypeStruct((n, heads, t, width), dtype),
                pl.BlockSpec((1, heads, tm, width), lambda i, j: (i, 0, j, 0)))

    def tmaj(width):
        return (jax.ShapeDtypeStruct((n, t, width), F32), pl.BlockSpec((1, tm, width), lambda i, j: (i, j, 0)))

    outs = [hm(HEAD_DIM, BF16), hm(HEAD_DIM, BF16), hm(HEAD_DIM, BF16), tmaj(GROUP), tmaj(GROUP),
            hm(LANES, BF16), tmaj(KV_LORA), tmaj(B_ROPE),
            hm(HEAD_DIM, BF16), hm(HEAD_DIM, BF16), hm(HEAD_DIM, BF16), tmaj(GROUP), tmaj(GROUP),
            hm(HEAD_DIM, BF16), hm(HEAD_DIM, F32, 4 * HEADS)]
    names = ('aq', 'ak', 'av', 'ak_s', 'av_s', 'bq', 'ckv', 'kr', 'cq', 'ck', 'cv', 'ck_s', 'cv_s', 'mq', 'gates')
    res = pl.pallas_call(
        _proj_kernel,
        grid=(n, nt),
        in_specs=[pl.BlockSpec((1, tm, D_MODEL), lambda i, j: (i, j, 0)), tab_spec, tab_spec, tab_spec]
        + [full(w) for w in weights],
        out_specs=[o[1] for o in outs],
        out_shape=[o[0] for o in outs],
        scratch_shapes=[pltpu.VMEM((tm, GROUP), F32)],
        compiler_params=_params(2),
        name='proj',
    )(x, c, s1, s2, *weights)
    return dict(zip(names, res))


def _kv_kernel(ckv_ref, kr_ref, wkv_ref, place_ref, gk_ref, gkg_ref, vone_ref, k_o, v_o):
    kv = _dot(ckv_ref[0].astype(BF16), wkv_ref[...])
    kr = _dot(kr_ref[0].astype(BF16), place_ref[...])
    for half in range(2):
        part = kv[:, half * 2 * LANES:(half + 1) * 2 * LANES]
        m = _group_mean(part * part, gk_ref[...])
        normed = part * lax.rsqrt(m + EPS) * gkg_ref[:, half * 2 * LANES:(half + 1) * 2 * LANES]
        for hh in range(2):
            k_o[0, 2 * half + hh] = (normed[:, hh * LANES:(hh + 1) * LANES] + kr).astype(BF16)
    for h in range(HEADS):
        v_o[0, h] = (kv[:, (HEADS + h) * LANES:(HEADS + h + 1) * LANES] + vone_ref[...]).astype(BF16)


def _kv_expand(ckv, kr, lw, tl):
    n, l, _ = ckv.shape
    weights = (lw['wkv'], lw['place'], lw['gk'], lw['gk_gain'], lw['v_one'])
    return pl.pallas_call(
        _kv_kernel,
        grid=(n, l // tl),
        in_specs=[pl.BlockSpec((1, tl, KV_LORA), lambda i, j: (i, j, 0)),
                  pl.BlockSpec((1, tl, B_ROPE), lambda i, j: (i, j, 0))]
        + [pl.BlockSpec(w.shape, lambda i, j: (0, 0)) for w in weights],
        out_specs=[pl.BlockSpec((1, HEADS, tl, LANES), lambda i, j: (i, 0, j, 0)),
                   pl.BlockSpec((1, HEADS, tl, LANES), lambda i, j: (i, 0, j, 0))],
        out_shape=[jax.ShapeDtypeStruct((n, HEADS, l, LANES), BF16)] * 2,
        compiler_params=_params(2),
        name='kv_expand',
    )(ckv, kr, *weights)


def _memkv_kernel(mem_ref, ng_ref, w_ref, g64_ref, kg_ref, k_o, v_o):
    x = mem_ref[0]
    xn = (x * lax.rsqrt(jnp.mean(x * x, axis=-1, keepdims=True) + EPS) * ng_ref[...]).astype(BF16)
    kv = _dot(xn, w_ref[...])
    k = kv[:, :GROUP]
    m = _group_mean(k * k, g64_ref[...])
    k_o[0] = k * lax.rsqrt(m + EPS) * kg_ref[...]
    v_o[0] = kv[:, GROUP:]


def _mem_kv(mem, lw):
    n, nm, _ = mem.shape
    weights = (lw['m_norm_g'], lw['w_mem_kv'], lw['g64'], lw['m_kn_g'])
    return pl.pallas_call(
        _memkv_kernel,
        grid=(n,),
        in_specs=[pl.BlockSpec((1, nm, D_MODEL), lambda i: (i, 0, 0))]
        + [pl.BlockSpec(w.shape, lambda i: (0, 0)) for w in weights],
        out_specs=[pl.BlockSpec((1, nm, GROUP), lambda i: (i, 0, 0))] * 2,
        out_shape=[jax.ShapeDtypeStruct((n, nm, GROUP), F32)] * 2,
        compiler_params=_params(1),
        name='mem_kv',
    )(mem, *weights)


def _band_kernel(q_ref, k_ref, v_ref, bias_ref, o_ref, *, rows, span, mask_front):
    start = pl.multiple_of(pl.program_id(2) * rows, CHUNK)
    k = k_ref[0, 0, pl.ds(start, span), :]
    v = v_ref[0, 0, pl.ds(start, span), :]
    s = _dot_nt(q_ref[0, 0], k) + bias_ref[0]
    if mask_front:
        kpos = start - A_WIN + lax.broadcasted_iota(jnp.int32, (rows, span), 1)
        s = jnp.where(kpos >= 0, s, NEG)
    p = jnp.exp(s - jnp.max(s, axis=-1, keepdims=True))
    l = jnp.sum(p, axis=-1, keepdims=True)
    o_ref[0, 0] = _dot(p.astype(BF16), v) / l


def _band_attn(q, k, v, bias, g, mask_front):
    n, _, t, _ = q.shape
    rows, span = CHUNK * g, CHUNK * (g + A_PREV_CHUNKS)
    lk = k.shape[2]
    return pl.pallas_call(
        functools.partial(_band_kernel, rows=rows, span=span, mask_front=mask_front),
        grid=(n, HEADS, t // rows),
        in_specs=[pl.BlockSpec((1, 1, rows, HEAD_DIM), lambda i, h, j: (i, h, j, 0)),
                  pl.BlockSpec((1, 1, lk, HEAD_DIM), lambda i, h, j: (i, h, 0, 0)),
                  pl.BlockSpec((1, 1, lk, HEAD_DIM), lambda i, h, j: (i, h, 0, 0)),
                  pl.BlockSpec((1, rows, span), lambda i, h, j: (h, 0, 0))],
        out_specs=pl.BlockSpec((1, 1, rows, HEAD_DIM), lambda i, h, j: (i, h, j, 0)),
        out_shape=jax.ShapeDtypeStruct((n, HEADS, t, HEAD_DIM), F32),
        compiler_params=_params(3),
        name='band_attn',
    )(q, k, v, bias)


def _band_bias(tab, g):
    rows, span = CHUNK * g, CHUNK * (g + A_PREV_CHUNKS)
    i = jnp.arange(rows)[:, None]
    j = jnp.arange(span)[None, :]
    rel = i + A_WIN - j
    lo = (i // CHUNK) * CHUNK
    inband = (j >= lo) & (j < lo + CHUNK * (A_PREV_CHUNKS + 1))
    b = tab[:, jnp.clip(rel, -REL_CLIP, REL_CLIP) + REL_CLIP].astype(F32)
    return jnp.where(inband[None], b, NEG)


def _softmax_kernel(q_ref, k_ref, v_ref, o_ref, *, hb, tq, tsup, dw, q0, causal, lk):
    q_start = pl.multiple_of(q0 + pl.program_id(2) * tq, CHUNK)

    def region(h, off, n, width, m, acc, masked):
        q = q_ref[0, h]
        scores = []
        for j in range(n):
            k = k_ref[0, h, pl.ds(pl.multiple_of(off + j * width, CHUNK), width), :]
            s = _dot_nt(q, k)
            if masked:
                qchunk = lax.broadcasted_iota(jnp.int32, (tq, width), 0) // CHUNK
                kchunk = (j * width + lax.broadcasted_iota(jnp.int32, (tq, width), 1)) // CHUNK
                s = jnp.where(kchunk <= qchunk, s, NEG)
            scores.append(s)
        m_new = m
        for s in scores:
            m_new = jnp.maximum(m_new, jnp.max(s, axis=-1, keepdims=True))
        pv = None
        for j, s in enumerate(scores):
            v = v_ref[0, h, pl.ds(pl.multiple_of(off + j * width, CHUNK), width), :]
            d = _dot(jnp.exp2(s - m_new).astype(BF16), v)
            pv = d if pv is None else pv + d
        return m_new, jnp.exp2(m - m_new) * acc + pv

    def sweep(i, carry):
        ms, accs = carry
        off = pl.multiple_of(i * tsup, tsup)
        out = [region(h, off, tsup // min(tsup, SUB), min(tsup, SUB), ms[h], accs[h], False) for h in range(hb)]
        return tuple(o[0] for o in out), tuple(o[1] for o in out)

    carry = (tuple(jnp.full((tq, 1), NEG, F32) for _ in range(hb)),
             tuple(jnp.zeros((tq, LANES), F32) for _ in range(hb)))
    carry = lax.fori_loop(0, q_start // tsup if causal else lk // tsup, sweep, carry)
    ms, accs = carry
    for h in range(hb):
        acc = accs[h]
        if causal:
            dsub = min(dw, SUB)
            _, acc = region(h, q_start, dw // dsub, dsub, ms[h], acc, True)
        o_ref[0, h] = (acc / acc[:, HEAD_DIM:HEAD_DIM + 1])[:, :HEAD_DIM]


def _softmax_attn(q, k, v, *, hb, tq, tsup, dw=0, q0=0, causal):
    n, _, t, dk = q.shape
    lk = k.shape[2]
    nq = t // tq
    assert t % tq == 0 and tq % CHUNK == 0 and q0 % CHUNK == 0 and HEADS % hb == 0
    if causal:
        assert q0 % tsup == 0 and (nq == 1 or tq % tsup == 0) and q0 + (nq - 1) * tq + dw <= lk
    else:
        assert lk % tsup == 0
    return pl.pallas_call(
        functools.partial(_softmax_kernel, hb=hb, tq=tq, tsup=tsup, dw=dw, q0=q0, causal=causal, lk=lk),
        grid=(n, HEADS // hb, nq),
        in_specs=[pl.BlockSpec((1, hb, tq, dk), lambda i, h, j: (i, h, j, 0)),
                  pl.BlockSpec((1, hb, lk, dk), lambda i, h, j: (i, h, 0, 0)),
                  pl.BlockSpec((1, hb, lk, LANES), lambda i, h, j: (i, h, 0, 0))],
        out_specs=pl.BlockSpec((1, hb, tq, HEAD_DIM), lambda i, h, j: (i, h, j, 0)),
        out_shape=jax.ShapeDtypeStruct((n, HEADS, t, HEAD_DIM), F32),
        compiler_params=_params(3),
        name='softmax_attn',
    )(q, k, v)


def _augment_v(v):
    n, h, l, _ = v.shape
    return jnp.concatenate([v, jnp.ones((n, h, l, 1), v.dtype), jnp.zeros((n, h, l, LANES - HEAD_DIM - 1), v.dtype)],
                           axis=-1)


def _stick_kernel(q_ref, k_ref, v_ref, tri_ref, o_ref, *, hb, tq, tsup, dw, q0):
    q_start = pl.multiple_of(q0 + pl.program_id(2) * tq, CHUNK)

    def region(h, off, n, width, run, acc, masked):
        tri = tri_ref[...] if width == SUB else tri_ref[0:width, 0:width]
        q = q_ref[0, h]
        parts = []
        for j in range(n):
            k = k_ref[0, h, pl.ds(pl.multiple_of(off + j * width, CHUNK), width), :]
            z = _dot_nt(q, k)
            log_1m = jnp.minimum(-z, 0.0) - jnp.log(1.0 + jnp.exp2(-jnp.abs(z))) * LOG2E
            mask = None
            if masked:
                mask = (j * width + lax.broadcasted_iota(jnp.int32, (tq, width), 1)
                        < lax.broadcasted_iota(jnp.int32, (tq, width), 0))
                log_1m = jnp.where(mask, log_1m, 0.0)
            parts.append((z, log_1m, _dot(log_1m.astype(BF16), tri), mask))
        for j in reversed(range(n)):
            z, log_1m, tail, mask = parts[j]
            w = jnp.exp2(z + log_1m + tail + run)
            if masked:
                w = jnp.where(mask, w, 0.0)
            v = v_ref[0, h, pl.ds(pl.multiple_of(off + j * width, CHUNK), width), :]
            acc = acc + _dot(w.astype(BF16), v)
            run = run + (tail[:, 0:1] + log_1m[:, 0:1])
        return run, acc

    runs, accs = [], []
    dsub = min(dw, SUB)
    for h in range(hb):
        run, acc = region(h, q_start, dw // dsub, dsub, jnp.zeros((tq, 1), F32), jnp.zeros((tq, HEAD_DIM), F32), True)
        runs.append(run)
        accs.append(acc)
    n_super = q_start // tsup

    def sweep(i, carry):
        runs, accs = carry
        off = pl.multiple_of((n_super - 1 - i) * tsup, tsup)
        out = [region(h, off, tsup // SUB, SUB, runs[h], accs[h], False) for h in range(hb)]
        return tuple(o[0] for o in out), tuple(o[1] for o in out)

    _, accs = lax.fori_loop(0, n_super, sweep, (tuple(runs), tuple(accs)))
    for h in range(hb):
        o_ref[0, h] = accs[h]


def _stick_attn(q, k, v, *, hb, tq, tsup, dw, q0=0):
    n, _, t, _ = q.shape
    lk = k.shape[2]
    nq = t // tq
    assert t % tq == 0 and q0 % CHUNK == 0 and tq % CHUNK == 0 and tsup % SUB == 0 and HEADS % hb == 0
    assert q0 % tsup == 0 and (nq == 1 or tq % tsup == 0) and q0 + (nq - 1) * tq + dw <= lk
    r = jnp.arange(SUB)
    tri = (r[:, None] > r[None, :]).astype(BF16)
    return pl.pallas_call(
        functools.partial(_stick_kernel, hb=hb, tq=tq, tsup=tsup, dw=dw, q0=q0),
        grid=(n, HEADS // hb, nq),
        in_specs=[pl.BlockSpec((1, hb, tq, HEAD_DIM), lambda i, h, j: (i, h, j, 0)),
                  pl.BlockSpec((1, hb, lk, HEAD_DIM), lambda i, h, j: (i, h, 0, 0)),
                  pl.BlockSpec((1, hb, lk, HEAD_DIM), lambda i, h, j: (i, h, 0, 0)),
                  pl.BlockSpec((SUB, SUB), lambda i, h, j: (0, 0))],
        out_specs=pl.BlockSpec((1, hb, tq, HEAD_DIM), lambda i, h, j: (i, h, j, 0)),
        out_shape=jax.ShapeDtypeStruct((n, HEADS, t, HEAD_DIM), F32),
        compiler_params=_params(3),
        name='stick_attn',
    )(q, k, v, tri)


def _merge_kernel(x_ref, oa_ref, ob_ref, oc_ref, om_ref, gate_ref, og_ref, w_ref, y_ref):
    acc = x_ref[0]
    for gi, o_ref in enumerate((oa_ref, ob_ref, oc_ref, om_ref)):
        for h in range(HEADS):
            j = HEADS * gi + h
            o = o_ref[0, h]
            y = o * lax.rsqrt(jnp.mean(o * o, axis=-1, keepdims=True) + EPS) * og_ref[j] * gate_ref[0, j]
            acc = acc + _dot(y.astype(BF16), w_ref[j])
    y_ref[0] = acc


def _merge(x, o_a, o_b, o_c, o_m, gates, lw, tm):
    n, t, _ = x.shape
    hspec = pl.BlockSpec((1, HEADS, tm, HEAD_DIM), lambda i, j: (i, 0, j, 0))
    return pl.pallas_call(
        _merge_kernel,
        grid=(n, t // tm),
        in_specs=[pl.BlockSpec((1, tm, D_MODEL), lambda i, j: (i, j, 0)), hspec, hspec, hspec, hspec,
                  pl.BlockSpec((1, 4 * HEADS, tm, HEAD_DIM), lambda i, j: (i, 0, j, 0)),
                  pl.BlockSpec((4 * HEADS, 1, HEAD_DIM), lambda i, j: (0, 0, 0)),
                  pl.BlockSpec((4 * HEADS, HEAD_DIM, D_MODEL), lambda i, j: (0, 0, 0))],
        out_specs=pl.BlockSpec((1, tm, D_MODEL), lambda i, j: (i, j, 0)),
        out_shape=jax.ShapeDtypeStruct((n, t, D_MODEL), F32),
        compiler_params=_params(2),
        name='merge',
    )(x, o_a, o_b, o_c, o_m, gates, lw['out_g'], lw['w_out'])


def _block_diag(block, reps):
    return jnp.kron(jnp.eye(reps, dtype=F32), block)


def _layer_weights(l, p):
    w_in = p['w_in'][l]
    zeros = lambda c: jnp.zeros((D_MODEL, c), F32)
    kr0 = 1408
    w_in = jnp.concatenate([w_in[:, :kr0], zeros(ROPE_LANE0), w_in[:, kr0:kr0 + B_ROPE],
                            zeros(LANES - ROPE_LANE0 - B_ROPE), w_in[:, kr0 + B_ROPE:]], axis=1)
    assert w_in.shape[1] == IN_WIDTH_PADDED
    ones = lambda k: jnp.full((k, k), 1.0 / k, F32)
    zpad = lambda k: jnp.zeros((k, k), F32)
    tile = lambda g, reps=HEADS: jnp.tile(g, reps)[None, :]
    pad_to = lambda g, k: jnp.concatenate([g, jnp.zeros((k - g.shape[0],), F32)])
    wq = jnp.pad(p['b_wq_b'][l].reshape(Q_LORA, HEADS, B_QK), ((0, 0), (0, 0), (0, LANES - B_QK)))
    wkv = p['b_wkv_b'][l].reshape(KV_LORA, HEADS, B_NOPE + HEAD_DIM)
    wk = jnp.pad(wkv[:, :, :B_NOPE], ((0, 0), (0, 0), (0, LANES - B_NOPE))).reshape(KV_LORA, HEADS * LANES)
    wv = jnp.pad(wkv[:, :, B_NOPE:], ((0, 0), (0, 0), (0, LANES - HEAD_DIM))).reshape(KV_LORA, HEADS * LANES)
    gq128 = jax.scipy.linalg.block_diag(ones(B_NOPE), ones(B_ROPE), zpad(LANES - B_QK))
    gk128 = jax.scipy.linalg.block_diag(ones(B_NOPE), zpad(LANES - B_NOPE))
    place = jnp.zeros((B_ROPE, LANES), F32).at[jnp.arange(B_ROPE), ROPE_LANE0 + jnp.arange(B_ROPE)].set(1.0)
    return dict(
        norm_g=p['norm_g'][l][None, :],
        w_in=w_in.astype(BF16),
        g64=_block_diag(ones(HEAD_DIM), HEADS).astype(BF16),
        a_qn_g=tile(p['a_qn_g'][l]), a_kn_g=tile(p['a_kn_g'][l]),
        b_cq_g=p['b_cq_g'][l][None, :],
        wq=wq.reshape(Q_LORA, HEADS * LANES).astype(BF16),
        gq=_block_diag(gq128, 2).astype(BF16),
        gq_gain=tile(jnp.concatenate([p['b_qn_g'][l], pad_to(p['b_qr_g'][l], LANES - B_NOPE)])),
        b_ckv_g=p['b_ckv_g'][l][None, :],
        kr_gain=jnp.concatenate([jnp.zeros((ROPE_LANE0,), F32), pad_to(p['b_kr_g'][l], LANES - ROPE_LANE0)])[None, :],
        m_qn_g=tile(p['m_qn_g'][l]),
        wkv=jnp.concatenate([wk, wv], axis=1).astype(BF16),
        place=place.astype(BF16),
        gk=_block_diag(gk128, 2).astype(BF16),
        gk_gain=tile(pad_to(p['b_kn_g'][l], LANES)),
        v_one=jnp.zeros((1, LANES), F32).at[0, HEAD_DIM].set(1.0),
        m_norm_g=p['m_norm_g'][l][None, :],
        w_mem_kv=p['w_mem_kv'][l].astype(BF16),
        m_kn_g=tile(p['m_kn_g'][l]),
        out_g=p['out_g'][l].reshape(4 * HEADS, 1, HEAD_DIM),
        w_out=p['w_out'][l].reshape(4 * HEADS, HEAD_DIM, D_MODEL).astype(BF16),
        a_rel_bias=p['a_rel_bias'][l],
    )


def _rope_tables(pos):
    half = B_ROPE // 2
    freqs = ROPE_THETA ** (-jnp.arange(half, dtype=F32) / half)
    ang = pos.astype(F32)[:, None] * freqs[None, :]
    cos, sin = jnp.cos(ang), jnp.sin(ang)
    t = pos.shape[0]
    one, zero = jnp.ones((t, ROPE_LANE0), F32), jnp.zeros((t, ROPE_LANE0), F32)
    tail, z16 = jnp.zeros((t, LANES - ROPE_LANE0 - B_ROPE), F32), jnp.zeros((t, half), F32)
    c = jnp.concatenate([one, cos, cos, tail], axis=1)
    s1 = jnp.concatenate([zero, -sin, z16, tail], axis=1)
    s2 = jnp.concatenate([zero, z16, sin, tail], axis=1)
    return c, s1, s2


def _head_major(a):
    return a.transpose(0, 2, 1, 3)


def _token_major(a):
    return a.reshape(a.shape[0], a.shape[1], HEADS, HEAD_DIM)


def _pad_rows(a, front, back):
    return jnp.pad(a, ((0, 0), (0, 0), (front, back), (0, 0)))


def _tiles(s):
    tm = min(256, s)
    tq = min(512, s)
    g = min(4, s // CHUNK)
    return tm, tq, g


def _layer_prompt(x, mem, lw, rope_tabs):
    n, s, _ = x.shape
    tm, tq, g = _tiles(s)
    pr = _proj(x, rope_tabs, lw, tm)
    kb, vb = _kv_expand(pr['ckv'], pr['kr'], lw, tm)
    bias = _band_bias(lw['a_rel_bias'], g)
    o_a = _band_attn(pr['aq'], _pad_rows(pr['ak'], A_WIN, 0), _pad_rows(pr['av'], A_WIN, 0), bias, g, True)
    o_b = _softmax_attn(pr['bq'], kb, vb, hb=1, tq=tq, tsup=tq, dw=tq, causal=True)
    o_c = _stick_attn(pr['cq'], pr['ck'], pr['cv'], hb=1, tq=tq, tsup=tq, dw=tq)
    mk, mv = _mem_kv(mem, lw)
    mk4, mv4 = _token_major(mk), _token_major(mv)
    o_m = _softmax_attn(pr['mq'], _head_major(mk4).astype(BF16), _augment_v(_head_major(mv4).astype(BF16)),
                        hb=1, tq=tq, tsup=mk.shape[1], causal=False)
    y = _merge(x, o_a, o_b, o_c, o_m, pr['gates'], lw, tm)
    keep = min(A_WIN, s)
    state = (_token_major(pr['ak_s'][:, s - keep:]), _token_major(pr['av_s'][:, s - keep:]), pr['ckv'], pr['kr'],
             _token_major(pr['ck_s']), _token_major(pr['cv_s']), mk4, mv4)
    return y, state


def _layer_step(x, lw, rope_tabs, ca_k, ca_v, cb_ckv, cb_kr, cc_k, cc_v, cm_k, cm_v):
    n, t, _ = x.shape
    n_past = cb_ckv.shape[1]
    assert t == CHUNK and n_past % CHUNK == 0 and ca_k.shape[1] == A_WIN
    dw = 2 * CHUNK
    back = dw - t
    pr = _proj(x, rope_tabs, lw, t)
    ckv_all = jnp.pad(jnp.concatenate([cb_ckv, pr['ckv']], 1), ((0, 0), (0, back), (0, 0)))
    kr_all = jnp.pad(jnp.concatenate([cb_kr, pr['kr']], 1), ((0, 0), (0, back), (0, 0)))
    kb, vb = _kv_expand(ckv_all, kr_all, lw, dw)
    bias = _band_bias(lw['a_rel_bias'], 1)
    ak_all = jnp.concatenate([_head_major(ca_k).astype(BF16), pr['ak']], 2)
    av_all = jnp.concatenate([_head_major(ca_v).astype(BF16), pr['av']], 2)
    o_a = _band_attn(pr['aq'], ak_all, av_all, bias, 1, False)
    o_b = _softmax_attn(pr['bq'], kb, vb, hb=HEADS, tq=t, tsup=n_past, dw=dw, q0=n_past, causal=True)
    ck_all = _pad_rows(jnp.concatenate([_head_major(cc_k).astype(BF16), pr['ck']], 2), 0, back)
    cv_all = _pad_rows(jnp.concatenate([_head_major(cc_v).astype(BF16), pr['cv']], 2), 0, back)
    o_c = _stick_attn(pr['cq'], ck_all, cv_all, hb=HEADS, tq=t, tsup=n_past, dw=dw, q0=n_past)
    o_m = _softmax_attn(pr['mq'], _head_major(cm_k).astype(BF16), _augment_v(_head_major(cm_v).astype(BF16)),
                        hb=HEADS, tq=t, tsup=cm_k.shape[1], causal=False)
    y = _merge(x, o_a, o_b, o_c, o_m, pr['gates'], lw, t)
    state = (jnp.concatenate([ca_k[:, t:], _token_major(pr['ak_s'])], 1),
             jnp.concatenate([ca_v[:, t:], _token_major(pr['av_s'])], 1),
             pr['ckv'], pr['kr'], _token_major(pr['ck_s']), _token_major(pr['cv_s']))
    return y, state


def kernel(x_prompt, x_sample, mem_prompt, cache_a_k, cache_a_v, cache_b_ckv, cache_b_krope, cache_c_k, cache_c_v,
           cache_mem_k, cache_mem_v, norm_g, w_in, a_qn_g, a_kn_g, a_rel_bias, b_cq_g, b_wq_b, b_ckv_g, b_wkv_b,
           b_qn_g, b_qr_g, b_kn_g, b_kr_g, m_norm_g, w_mem_kv, m_qn_g, m_kn_g, out_g, w_out):
    p = dict(norm_g=norm_g, w_in=w_in, a_qn_g=a_qn_g, a_kn_g=a_kn_g, a_rel_bias=a_rel_bias, b_cq_g=b_cq_g,
             b_wq_b=b_wq_b, b_ckv_g=b_ckv_g, b_wkv_b=b_wkv_b, b_qn_g=b_qn_g, b_qr_g=b_qr_g, b_kn_g=b_kn_g,
             b_kr_g=b_kr_g, m_norm_g=m_norm_g, w_mem_kv=w_mem_kv, m_qn_g=m_qn_g, m_kn_g=m_kn_g, out_g=out_g,
             w_out=w_out)
    depth = w_in.shape[0]
    s, t, n_past = x_prompt.shape[1], x_sample.shape[1], cache_b_ckv.shape[2]
    tabs_p = _rope_tables(jnp.arange(s))
    tabs_s = _rope_tables(n_past + jnp.arange(t))
    hp, hs = x_prompt, x_sample
    p_states, s_states = [], []
    for l in range(depth):
        lw = _layer_weights(l, p)
        hp, sp = _layer_prompt(hp, mem_prompt, lw, tabs_p)
        hs, ss = _layer_step(hs, lw, tabs_s, cache_a_k[l], cache_a_v[l], cache_b_ckv[l], cache_b_krope[l],
                             cache_c_k[l], cache_c_v[l], cache_mem_k[l], cache_mem_v[l])
        p_states.append(sp)
        s_states.append(ss)
    outs_p = [jnp.stack([st[i] for st in p_states]) for i in range(8)]
    outs_s = [jnp.stack([st[i] for st in s_states]) for i in range(6)]
    return (hp, hs, *outs_p, *outs_s)
```

```python
import functools

import jax
import jax.numpy as jnp
from jax import lax
from jax.experimental import pallas as pl
from jax.experimental.pallas import tpu as pltpu

F32 = jnp.float32
BF16 = jnp.bfloat16

D_MODEL = 1024
HEAD_DIM = 64
HEADS = 4
GROUP = HEADS * HEAD_DIM
CHUNK = 64
A_PREV_CHUNKS = 8
A_WIN = A_PREV_CHUNKS * CHUNK
REL_CLIP = 256
B_NOPE = 64
B_ROPE = 32
B_QK = B_NOPE + B_ROPE
Q_LORA = 256
KV_LORA = 128
ROPE_THETA = 10000.0
MLA_SCALE = B_QK ** -0.5
HEAD_SCALE = HEAD_DIM ** -0.5
EPS = 1e-6
NEG = -1e30
LOG2E = 1.4426950408889634
SIGN_BIT = -2 ** 31
SUB = 256
PROMPT_HB = 4
LANES = 128
VMEM_LIMIT = 48 * 1024 * 1024

OFF_AQ, OFF_AK, OFF_AV, OFF_AG = 0, 256, 512, 768
OFF_BCQ, OFF_BCKV, OFF_BKR, OFF_BG = 1024, 1280, 1408, 1536
OFF_CQ, OFF_CK, OFF_CV, OFF_CG = 1792, 2048, 2304, 2560
OFF_MQ, OFF_MG = 2816, 3072
IN_WIDTH_PADDED = 3328
ROPE_LANE0 = B_NOPE


def _params(n_axes):
    return pltpu.CompilerParams(dimension_semantics=("arbitrary",) * n_axes, vmem_limit_bytes=VMEM_LIMIT)


def _split_bf16(x):
    hi = x.astype(BF16)
    lo = (x - hi.astype(F32)).astype(BF16)
    return hi, lo


def _dot(a, b):
    return jnp.dot(a, b, preferred_element_type=F32)


def _dot_nt(a, b):
    return lax.dot_general(a, b, (((1,), (1,)), ((), ())), preferred_element_type=F32)


def _group_mean(x2, g):
    hi, lo = _split_bf16(x2)
    return _dot(hi, g) + _dot(lo, g)


def _rope(x, c, s1, s2):
    return x * c + pltpu.roll(x, LANES - B_ROPE // 2, 1) * s1 + pltpu.roll(x, B_ROPE // 2, 1) * s2


def _proj_kernel(x_ref, c_ref, s1_ref, s2_ref, ng_ref, win_ref, g64_ref, aqg_ref, akg_ref, bcqg_ref, wq_ref,
                 gq_ref, gqg_ref, ckvg_ref, krg_ref, mqg_ref,
                 aq_o, ak_o, av_o, aks_o, avs_o, bq_o, ckv_o, kr_o, cq_o, ck_o, cv_o, cks_o, cvs_o, mq_o, gate_o,
                 scr):
    x = x_ref[0]
    ms = jnp.mean(x * x, axis=-1, keepdims=True)
    xn = (x * lax.rsqrt(ms + EPS) * ng_ref[...]).astype(BF16)
    c, s1, s2 = c_ref[...], s1_ref[...], s2_ref[...]

    def seg(off, width):
        return _dot(xn, win_ref[:, off:off + width])

    def head_rms(h, gain_ref):
        m = _group_mean(h * h, g64_ref[...])
        return h * lax.rsqrt(m + EPS) * gain_ref[...]

    def put_heads(o_ref, val, base=0):
        scr[...] = val
        for h in range(HEADS):
            o_ref[0, base + h] = scr[:, h * HEAD_DIM:(h + 1) * HEAD_DIM].astype(o_ref.dtype)

    put_heads(aq_o, head_rms(seg(OFF_AQ, GROUP), aqg_ref) * HEAD_SCALE)
    ak = head_rms(seg(OFF_AK, GROUP), akg_ref)
    aks_o[0] = ak
    put_heads(ak_o, ak)
    av = seg(OFF_AV, GROUP)
    avs_o[0] = av
    put_heads(av_o, av)

    for gi, off in enumerate((OFF_AG, OFF_BG, OFF_CG, OFF_MG)):
        g = seg(off, GROUP)
        put_heads(gate_o, g * jax.nn.sigmoid(g), base=HEADS * gi)

    h = seg(OFF_BCQ, Q_LORA)
    cq = (h * lax.rsqrt(jnp.mean(h * h, axis=-1, keepdims=True) + EPS) * bcqg_ref[...]).astype(BF16)
    qb = _dot(cq, wq_ref[...])
    for half in range(2):
        part = qb[:, half * 2 * LANES:(half + 1) * 2 * LANES]
        m = _group_mean(part * part, gq_ref[...])
        normed = part * lax.rsqrt(m + EPS) * gqg_ref[:, half * 2 * LANES:(half + 1) * 2 * LANES]
        for hh in range(2):
            xh = normed[:, hh * LANES:(hh + 1) * LANES]
            bq_o[0, 2 * half + hh] = (_rope(xh, c, s1, s2) * (MLA_SCALE * LOG2E)).astype(BF16)

    h = seg(OFF_BCKV, KV_LORA)
    ckv_o[0] = h * lax.rsqrt(jnp.mean(h * h, axis=-1, keepdims=True) + EPS) * ckvg_ref[...]
    h = seg(OFF_BKR, LANES)
    ms = jnp.sum(h * h, axis=-1, keepdims=True) * (1.0 / B_ROPE)
    kr = _rope(h * lax.rsqrt(ms + EPS) * krg_ref[...], c, s1, s2)
    scr[:, 0:LANES] = kr
    kr_o[0] = scr[:, ROPE_LANE0:ROPE_LANE0 + B_ROPE]

    put_heads(cq_o, seg(OFF_CQ, GROUP) * (-HEAD_SCALE * LOG2E))
    ck = seg(OFF_CK, GROUP)
    cks_o[0] = ck
    put_heads(ck_o, ck)
    cv = seg(OFF_CV, GROUP)
    cvs_o[0] = cv
    put_heads(cv_o, cv)

    put_heads(mq_o, head_rms(seg(OFF_MQ, GROUP), mqg_ref) * (HEAD_SCALE * LOG2E))


def _proj(x, rope_tabs, lw, tm):
    n, t, _ = x.shape
    nt = t // tm
    c, s1, s2 = rope_tabs

    def full(a):
        return pl.BlockSpec(a.shape, lambda i, j: (0,) * a.ndim)

    tab_spec = pl.BlockSpec((tm, LANES), lambda i, j: (j, 0))
    weights = (lw['norm_g'], lw['w_in'], lw['g64'], lw['a_qn_g'], lw['a_kn_g'], lw['b_cq_g'], lw['wq'], lw['gq'],
               lw['gq_gain'], lw['b_ckv_g'], lw['kr_gain'], lw['m_qn_g'])

    def hm(width, dtype, heads=HEADS):
        return (jax.ShapeDtypeStruct((n, heads, t, width), dtype),
                pl.BlockSpec((1, heads, tm, width), lambda i, j: (i, 0, j, 0)))

    def tmaj(width):
        return (jax.ShapeDtypeStruct((n, t, width), F32), pl.BlockSpec((1, tm, width), lambda i, j: (i, j, 0)))

    outs = [hm(HEAD_DIM, BF16), hm(HEAD_DIM, BF16), hm(HEAD_DIM, BF16), tmaj(GROUP), tmaj(GROUP),
            hm(LANES, BF16), tmaj(KV_LORA), tmaj(B_ROPE),
            hm(HEAD_DIM, BF16), hm(HEAD_DIM, BF16), hm(HEAD_DIM, BF16), tmaj(GROUP), tmaj(GROUP),
            hm(HEAD_DIM, BF16), hm(HEAD_DIM, F32, 4 * HEADS)]
    names = ('aq', 'ak', 'av', 'ak_s', 'av_s', 'bq', 'ckv', 'kr', 'cq', 'ck', 'cv', 'ck_s', 'cv_s', 'mq', 'gates')
    res = pl.pallas_call(
        _proj_kernel,
        grid=(n, nt),
        in_specs=[pl.BlockSpec((1, tm, D_MODEL), lambda i, j: (i, j, 0)), tab_spec, tab_spec, tab_spec]
        + [full(w) for w in weights],
        out_specs=[o[1] for o in outs],
        out_shape=[o[0] for o in outs],
        scratch_shapes=[pltpu.VMEM((tm, GROUP), F32)],
        compiler_params=_params(2),
        name='proj',
    )(x, c, s1, s2, *weights)
    return dict(zip(names, res))


def _kv_kernel(ckv_ref, kr_ref, wkv_ref, place_ref, gk_ref, gkg_ref, vone_ref, k_o, v_o):
    kv = _dot(ckv_ref[0].astype(BF16), wkv_ref[...])
    kr = _dot(kr_ref[0].astype(BF16), place_ref[...])
    for half in range(2):
        part = kv[:, half * 2 * LANES:(half + 1) * 2 * LANES]
        m = _group_mean(part * part, gk_ref[...])
        normed = part * lax.rsqrt(m + EPS) * gkg_ref[:, half * 2 * LANES:(half + 1) * 2 * LANES]
        for hh in range(2):
            k_o[0, 2 * half + hh] = (normed[:, hh * LANES:(hh + 1) * LANES] + kr).astype(BF16)
    for h in range(HEADS):
        v_o[0, h] = (kv[:, (HEADS + h) * LANES:(HEADS + h + 1) * LANES] + vone_ref[...]).astype(BF16)


def _kv_expand(ckv, kr, lw, tl):
    n, l, _ = ckv.shape
    weights = (lw['wkv'], lw['place'], lw['gk'], lw['gk_gain'], lw['v_one'])
    return pl.pallas_call(
        _kv_kernel,
        grid=(n, l // tl),
        in_specs=[pl.BlockSpec((1, tl, KV_LORA), lambda i, j: (i, j, 0)),
                  pl.BlockSpec((1, tl, B_ROPE), lambda i, j: (i, j, 0))]
        + [pl.BlockSpec(w.shape, lambda i, j: (0, 0)) for w in weights],
        out_specs=[pl.BlockSpec((1, HEADS, tl, LANES), lambda i, j: (i, 0, j, 0)),
                   pl.BlockSpec((1, HEADS, tl, LANES), lambda i, j: (i, 0, j, 0))],
        out_shape=[jax.ShapeDtypeStruct((n, HEADS, l, LANES), BF16)] * 2,
        compiler_params=_params(2),
        name='kv_expand',
    )(ckv, kr, *weights)


def _memkv_kernel(mem_ref, ng_ref, w_ref, g64_ref, kg_ref, k_o, v_o):
    x = mem_ref[0]
    xn = (x * lax.rsqrt(jnp.mean(x * x, axis=-1, keepdims=True) + EPS) * ng_ref[...]).astype(BF16)
    kv = _dot(xn, w_ref[...])
    k = kv[:, :GROUP]
    m = _group_mean(k * k, g64_ref[...])
    k_o[0] = k * lax.rsqrt(m + EPS) * kg_ref[...]
    v_o[0] = kv[:, GROUP:]


def _mem_kv(mem, lw):
    n, nm, _ = mem.shape
    weights = (lw['m_norm_g'], lw['w_mem_kv'], lw['g64'], lw['m_kn_g'])
    return pl.pallas_call(
        _memkv_kernel,
        grid=(n,),
        in_specs=[pl.BlockSpec((1, nm, D_MODEL), lambda i: (i, 0, 0))]
        + [pl.BlockSpec(w.shape, lambda i: (0, 0)) for w in weights],
        out_specs=[pl.BlockSpec((1, nm, GROUP), lambda i: (i, 0, 0))] * 2,
        out_shape=[jax.ShapeDtypeStruct((n, nm, GROUP), F32)] * 2,
        compiler_params=_params(1),
        name='mem_kv',
    )(mem, *weights)


def _band_kernel(q_ref, k_ref, v_ref, bias_ref, o_ref, *, rows, span, mask_front):
    start = pl.multiple_of(pl.program_id(2) * rows, CHUNK)
    k = k_ref[0, 0, pl.ds(start, span), :]
    v = v_ref[0, 0, pl.ds(start, span), :]
    s = _dot_nt(q_ref[0, 0], k) + bias_ref[0]
    if mask_front:
        kpos = start - A_WIN + lax.broadcasted_iota(jnp.int32, (rows, span), 1)
        s = jnp.where(kpos >= 0, s, NEG)
    p = jnp.exp(s - jnp.max(s, axis=-1, keepdims=True))
    l = jnp.sum(p, axis=-1, keepdims=True)
    o_ref[0, 0] = _dot(p.astype(BF16), v) / l


def _band_attn(q, k, v, bias, g, mask_front):
    n, _, t, _ = q.shape
    rows, span = CHUNK * g, CHUNK * (g + A_PREV_CHUNKS)
    lk = k.shape[2]
    return pl.pallas_call(
        functools.partial(_band_kernel, rows=rows, span=span, mask_front=mask_front),
        grid=(n, HEADS, t // rows),
        in_specs=[pl.BlockSpec((1, 1, rows, HEAD_DIM), lambda i, h, j: (i, h, j, 0)),
                  pl.BlockSpec((1, 1, lk, HEAD_DIM), lambda i, h, j: (i, h, 0, 0)),
                  pl.BlockSpec((1, 1, lk, HEAD_DIM), lambda i, h, j: (i, h, 0, 0)),
                  pl.BlockSpec((1, rows, span), lambda i, h, j: (h, 0, 0))],
        out_specs=pl.BlockSpec((1, 1, rows, HEAD_DIM), lambda i, h, j: (i, h, j, 0)),
        out_shape=jax.ShapeDtypeStruct((n, HEADS, t, HEAD_DIM), F32),
        compiler_params=_params(3),
        name='band_attn',
    )(q, k, v, bias)


def _band_bias(tab, g):
    rows, span = CHUNK * g, CHUNK * (g + A_PREV_CHUNKS)
    i = jnp.arange(rows)[:, None]
    j = jnp.arange(span)[None, :]
    lo = (i // CHUNK) * CHUNK
    inband = (j >= lo) & (j < lo + CHUNK * (A_PREV_CHUNKS + 1))
    period = rows + span - 1
    u = jnp.concatenate([jnp.arange(span), jnp.arange(-(rows - 1), 0)])
    line = tab[:, jnp.clip(A_WIN - u, -REL_CLIP, REL_CLIP) + REL_CLIP].astype(F32)
    skew = jnp.tile(line, (1, rows))[:, :rows * (period - 1)].reshape(-1, rows, period - 1)
    return jnp.where(inband[None], skew[:, :, :span], NEG)


def _resident_spec(block, nq):
    mode = dict(pipeline_mode=pl.Buffered(1)) if nq > 1 else {}
    return pl.BlockSpec(block, lambda i, h, j: (i, h, 0, 0), **mode)


def _softmax_kernel(q_ref, k_ref, v_ref, o_ref, *, hb, tq, tsup, dw, q0, causal, lk):
    q_start = pl.multiple_of(q0 + pl.program_id(2) * tq, CHUNK)

    def region(h, off, n, width, m, acc, masked):
        q = q_ref[0, h]
        scores = []
        for j in range(n):
            k = k_ref[0, h, pl.ds(pl.multiple_of(off + j * width, CHUNK), width), :]
            s = _dot_nt(q, k)
            if masked:
                qchunk = lax.broadcasted_iota(jnp.int32, (tq, width), 0) // CHUNK
                kchunk = (j * width + lax.broadcasted_iota(jnp.int32, (tq, width), 1)) // CHUNK
                s = jnp.where(kchunk <= qchunk, s, NEG)
            scores.append(s)
        m_new = m
        for s in scores:
            m_new = jnp.maximum(m_new, jnp.max(s, axis=-1, keepdims=True))
        pv = None
        for j, s in enumerate(scores):
            v = v_ref[0, h, pl.ds(pl.multiple_of(off + j * width, CHUNK), width), :]
            d = _dot(jnp.exp2(s - m_new).astype(BF16), v)
            pv = d if pv is None else pv + d
        return m_new, jnp.exp2(m - m_new) * acc + pv

    def sweep(i, carry):
        ms, accs = carry
        off = pl.multiple_of(i * tsup, tsup)
        out = [region(h, off, tsup // min(tsup, SUB), min(tsup, SUB), ms[h], accs[h], False) for h in range(hb)]
        return tuple(o[0] for o in out), tuple(o[1] for o in out)

    carry = (tuple(jnp.full((tq, 1), NEG, F32) for _ in range(hb)),
             tuple(jnp.zeros((tq, LANES), F32) for _ in range(hb)))
    carry = lax.fori_loop(0, q_start // tsup if causal else lk // tsup, sweep, carry)
    ms, accs = carry
    for h in range(hb):
        acc = accs[h]
        if causal:
            dsub = min(dw, SUB)
            _, acc = region(h, q_start, dw // dsub, dsub, ms[h], acc, True)
        o_ref[0, h] = (acc / acc[:, HEAD_DIM:HEAD_DIM + 1])[:, :HEAD_DIM]


def _softmax_attn(q, k, v, *, hb, tq, tsup, dw=0, q0=0, causal):
    n, _, t, dk = q.shape
    lk = k.shape[2]
    nq = t // tq
    assert t % tq == 0 and tq % CHUNK == 0 and q0 % CHUNK == 0 and HEADS % hb == 0
    if causal:
        assert q0 % tsup == 0 and (nq == 1 or tq % tsup == 0) and q0 + (nq - 1) * tq + dw <= lk
    else:
        assert lk % tsup == 0
    return pl.pallas_call(
        functools.partial(_softmax_kernel, hb=hb, tq=tq, tsup=tsup, dw=dw, q0=q0, causal=causal, lk=lk),
        grid=(n, HEADS // hb, nq),
        in_specs=[pl.BlockSpec((1, hb, tq, dk), lambda i, h, j: (i, h, j, 0)),
                  _resident_spec((1, hb, lk, dk), nq), _resident_spec((1, hb, lk, LANES), nq)],
        out_specs=pl.BlockSpec((1, hb, tq, HEAD_DIM), lambda i, h, j: (i, h, j, 0)),
        out_shape=jax.ShapeDtypeStruct((n, HEADS, t, HEAD_DIM), F32),
        compiler_params=_params(3),
        name='softmax_attn',
    )(q, k, v)


def _augment_v(v):
    n, h, l, _ = v.shape
    return jnp.concatenate([v, jnp.ones((n, h, l, 1), v.dtype), jnp.zeros((n, h, l, LANES - HEAD_DIM - 1), v.dtype)],
                           axis=-1)


def _stick_kernel(q_ref, k_ref, v_ref, tri_ref, o_ref, *, hb, tq, tsup, dw, q0):
    q_start = pl.multiple_of(q0 + pl.program_id(2) * tq, CHUNK)

    def region(h, off, n, width, run, acc, masked):
        tri = tri_ref[...] if width == SUB else tri_ref[0:width, 0:width]
        q = q_ref[0, h]
        parts = []
        for j in range(n):
            k = k_ref[0, h, pl.ds(pl.multiple_of(off + j * width, CHUNK), width), :]
            nz = _dot_nt(q, k)
            neg_abs = lax.bitcast_convert_type(lax.bitcast_convert_type(nz, jnp.int32) | SIGN_BIT, F32)
            log_1m = jnp.minimum(nz, 0.0) - jnp.log(1.0 + jnp.exp2(neg_abs)) * LOG2E
            mask = None
            if masked:
                mask = (j * width + lax.broadcasted_iota(jnp.int32, (tq, width), 1)
                        < lax.broadcasted_iota(jnp.int32, (tq, width), 0))
                log_1m = jnp.where(mask, log_1m, 0.0)
            parts.append((nz, log_1m, _dot(log_1m.astype(BF16), tri), mask))
        for j in reversed(range(n)):
            nz, log_1m, tail, mask = parts[j]
            w = jnp.exp2(log_1m - nz + tail)
            if masked:
                w = jnp.where(mask, w, 0.0)
            v = v_ref[0, h, pl.ds(pl.multiple_of(off + j * width, CHUNK), width), :]
            acc = acc + _dot(w.astype(BF16), v) * jnp.exp2(run)
            run = run + (tail[:, 0:1] + log_1m[:, 0:1])
        return run, acc

    runs, accs = [], []
    dsub = min(dw, SUB)
    for h in range(hb):
        run, acc = region(h, q_start, dw // dsub, dsub, jnp.zeros((tq, 1), F32), jnp.zeros((tq, HEAD_DIM), F32), True)
        runs.append(run)
        accs.append(acc)
    n_super = q_start // tsup

    def sweep(i, carry):
        runs, accs = carry
        off = pl.multiple_of((n_super - 1 - i) * tsup, tsup)
        out = [region(h, off, tsup // SUB, SUB, runs[h], accs[h], False) for h in range(hb)]
        return tuple(o[0] for o in out), tuple(o[1] for o in out)

    _, accs = lax.fori_loop(0, n_super, sweep, (tuple(runs), tuple(accs)))
    for h in range(hb):
        o_ref[0, h] = accs[h]


def _stick_attn(q, k, v, *, hb, tq, tsup, dw, q0=0):
    n, _, t, _ = q.shape
    lk = k.shape[2]
    nq = t // tq
    assert t % tq == 0 and q0 % CHUNK == 0 and tq % CHUNK == 0 and tsup % SUB == 0 and HEADS % hb == 0
    assert q0 % tsup == 0 and (nq == 1 or tq % tsup == 0) and q0 + (nq - 1) * tq + dw <= lk
    r = jnp.arange(SUB)
    tri = (r[:, None] > r[None, :]).astype(BF16)
    return pl.pallas_call(
        functools.partial(_stick_kernel, hb=hb, tq=tq, tsup=tsup, dw=dw, q0=q0),
        grid=(n, HEADS // hb, nq),
        in_specs=[pl.BlockSpec((1, hb, tq, HEAD_DIM), lambda i, h, j: (i, h, j, 0)),
                  _resident_spec((1, hb, lk, HEAD_DIM), nq), _resident_spec((1, hb, lk, HEAD_DIM), nq),
                  pl.BlockSpec((SUB, SUB), lambda i, h, j: (0, 0))],
        out_specs=pl.BlockSpec((1, hb, tq, HEAD_DIM), lambda i, h, j: (i, h, j, 0)),
        out_shape=jax.ShapeDtypeStruct((n, HEADS, t, HEAD_DIM), F32),
        compiler_params=_params(3),
        name='stick_attn',
    )(q, k, v, tri)


def _merge_kernel(x_ref, oa_ref, ob_ref, oc_ref, om_ref, gate_ref, og_ref, w_ref, y_ref):
    acc = x_ref[0]
    for gi, o_ref in enumerate((oa_ref, ob_ref, oc_ref, om_ref)):
        for h in range(HEADS):
            j = HEADS * gi + h
            o = o_ref[0, h]
            y = o * lax.rsqrt(jnp.mean(o * o, axis=-1, keepdims=True) + EPS) * og_ref[j] * gate_ref[0, j]
            acc = acc + _dot(y.astype(BF16), w_ref[j])
    y_ref[0] = acc


def _merge(x, o_a, o_b, o_c, o_m, gates, lw, tm):
    n, t, _ = x.shape
    hspec = pl.BlockSpec((1, HEADS, tm, HEAD_DIM), lambda i, j: (i, 0, j, 0))
    return pl.pallas_call(
        _merge_kernel,
        grid=(n, t // tm),
        in_specs=[pl.BlockSpec((1, tm, D_MODEL), lambda i, j: (i, j, 0)), hspec, hspec, hspec, hspec,
                  pl.BlockSpec((1, 4 * HEADS, tm, HEAD_DIM), lambda i, j: (i, 0, j, 0)),
                  pl.BlockSpec((4 * HEADS, 1, HEAD_DIM), lambda i, j: (0, 0, 0)),
                  pl.BlockSpec((4 * HEADS, HEAD_DIM, D_MODEL), lambda i, j: (0, 0, 0))],
        out_specs=pl.BlockSpec((1, tm, D_MODEL), lambda i, j: (i, j, 0)),
        out_shape=jax.ShapeDtypeStruct((n, t, D_MODEL), F32),
        compiler_params=_params(2),
        name='merge',
    )(x, o_a, o_b, o_c, o_m, gates, lw['out_g'], lw['w_out'])


def _block_diag(block, reps):
    return jnp.kron(jnp.eye(reps, dtype=F32), block)


def _layer_weights(l, p):
    w_in = p['w_in'][l]
    zeros = lambda c: jnp.zeros((D_MODEL, c), F32)
    kr0 = 1408
    w_in = jnp.concatenate([w_in[:, :kr0], zeros(ROPE_LANE0), w_in[:, kr0:kr0 + B_ROPE],
                            zeros(LANES - ROPE_LANE0 - B_ROPE), w_in[:, kr0 + B_ROPE:]], axis=1)
    assert w_in.shape[1] == IN_WIDTH_PADDED
    ones = lambda k: jnp.full((k, k), 1.0 / k, F32)
    zpad = lambda k: jnp.zeros((k, k), F32)
    tile = lambda g, reps=HEADS: jnp.tile(g, reps)[None, :]
    pad_to = lambda g, k: jnp.concatenate([g, jnp.zeros((k - g.shape[0],), F32)])
    wq = jnp.pad(p['b_wq_b'][l].reshape(Q_LORA, HEADS, B_QK), ((0, 0), (0, 0), (0, LANES - B_QK)))
    wkv = p['b_wkv_b'][l].reshape(KV_LORA, HEADS, B_NOPE + HEAD_DIM)
    wk = jnp.pad(wkv[:, :, :B_NOPE], ((0, 0), (0, 0), (0, LANES - B_NOPE))).reshape(KV_LORA, HEADS * LANES)
    wv = jnp.pad(wkv[:, :, B_NOPE:], ((0, 0), (0, 0), (0, LANES - HEAD_DIM))).reshape(KV_LORA, HEADS * LANES)
    gq128 = jax.scipy.linalg.block_diag(ones(B_NOPE), ones(B_ROPE), zpad(LANES - B_QK))
    gk128 = jax.scipy.linalg.block_diag(ones(B_NOPE), zpad(LANES - B_NOPE))
    place = jnp.zeros((B_ROPE, LANES), F32).at[jnp.arange(B_ROPE), ROPE_LANE0 + jnp.arange(B_ROPE)].set(1.0)
    return dict(
        norm_g=p['norm_g'][l][None, :],
        w_in=w_in.astype(BF16),
        g64=_block_diag(ones(HEAD_DIM), HEADS).astype(BF16),
        a_qn_g=tile(p['a_qn_g'][l]), a_kn_g=tile(p['a_kn_g'][l]),
        b_cq_g=p['b_cq_g'][l][None, :],
        wq=wq.reshape(Q_LORA, HEADS * LANES).astype(BF16),
        gq=_block_diag(gq128, 2).astype(BF16),
        gq_gain=tile(jnp.concatenate([p['b_qn_g'][l], pad_to(p['b_qr_g'][l], LANES - B_NOPE)])),
        b_ckv_g=p['b_ckv_g'][l][None, :],
        kr_gain=jnp.concatenate([jnp.zeros((ROPE_LANE0,), F32), pad_to(p['b_kr_g'][l], LANES - ROPE_LANE0)])[None, :],
        m_qn_g=tile(p['m_qn_g'][l]),
        wkv=jnp.concatenate([wk, wv], axis=1).astype(BF16),
        place=place.astype(BF16),
        gk=_block_diag(gk128, 2).astype(BF16),
        gk_gain=tile(pad_to(p['b_kn_g'][l], LANES)),
        v_one=jnp.zeros((1, LANES), F32).at[0, HEAD_DIM].set(1.0),
        m_norm_g=p['m_norm_g'][l][None, :],
        w_mem_kv=p['w_mem_kv'][l].astype(BF16),
        m_kn_g=tile(p['m_kn_g'][l]),
        out_g=p['out_g'][l].reshape(4 * HEADS, 1, HEAD_DIM),
        w_out=p['w_out'][l].reshape(4 * HEADS, HEAD_DIM, D_MODEL).astype(BF16),
        a_rel_bias=p['a_rel_bias'][l],
    )


def _rope_tables(pos):
    half = B_ROPE // 2
    freqs = ROPE_THETA ** (-jnp.arange(half, dtype=F32) / half)
    ang = pos.astype(F32)[:, None] * freqs[None, :]
    cos, sin = jnp.cos(ang), jnp.sin(ang)
    t = pos.shape[0]
    one, zero = jnp.ones((t, ROPE_LANE0), F32), jnp.zeros((t, ROPE_LANE0), F32)
    tail, z16 = jnp.zeros((t, LANES - ROPE_LANE0 - B_ROPE), F32), jnp.zeros((t, half), F32)
    c = jnp.concatenate([one, cos, cos, tail], axis=1)
    s1 = jnp.concatenate([zero, -sin, z16, tail], axis=1)
    s2 = jnp.concatenate([zero, z16, sin, tail], axis=1)
    return c, s1, s2


def _head_major(a):
    return a.transpose(0, 2, 1, 3)


def _token_major(a):
    return a.reshape(a.shape[0], a.shape[1], HEADS, HEAD_DIM)


def _pad_rows(a, front, back):
    return jnp.pad(a, ((0, 0), (0, 0), (front, back), (0, 0)))


def _tiles(s):
    tm = min(256, s)
    tq = min(512, s)
    g = min(4, s // CHUNK)
    return tm, tq, g


def _layer_prompt(x, mem, lw, rope_tabs):
    n, s, _ = x.shape
    tm, tq, g = _tiles(s)
    pr = _proj(x, rope_tabs, lw, tm)
    kb, vb = _kv_expand(pr['ckv'], pr['kr'], lw, tm)
    bias = _band_bias(lw['a_rel_bias'], g)
    o_a = _band_attn(pr['aq'], _pad_rows(pr['ak'], A_WIN, 0), _pad_rows(pr['av'], A_WIN, 0), bias, g, True)
    o_b = _softmax_attn(pr['bq'], kb, vb, hb=PROMPT_HB, tq=tq, tsup=tq, dw=tq, causal=True)
    o_c = _stick_attn(pr['cq'], pr['ck'], pr['cv'], hb=PROMPT_HB, tq=tq, tsup=tq, dw=tq)
    mk, mv = _mem_kv(mem, lw)
    mk4, mv4 = _token_major(mk), _token_major(mv)
    o_m = _softmax_attn(pr['mq'], _head_major(mk4).astype(BF16), _augment_v(_head_major(mv4).astype(BF16)),
                        hb=1, tq=tq, tsup=mk.shape[1], causal=False)
    y = _merge(x, o_a, o_b, o_c, o_m, pr['gates'], lw, tm)
    keep = min(A_WIN, s)
    state = (_token_major(pr['ak_s'][:, s - keep:]), _token_major(pr['av_s'][:, s - keep:]), pr['ckv'], pr['kr'],
             _token_major(pr['ck_s']), _token_major(pr['cv_s']), mk4, mv4)
    return y, state


def _layer_step(x, lw, rope_tabs, ca_k, ca_v, cb_ckv, cb_kr, cc_k, cc_v, cm_k, cm_v):
    n, t, _ = x.shape
    n_past = cb_ckv.shape[1]
    assert t == CHUNK and n_past % CHUNK == 0 and ca_k.shape[1] == A_WIN
    dw = 2 * CHUNK
    back = dw - t
    pr = _proj(x, rope_tabs, lw, t)
    ckv_all = jnp.pad(jnp.concatenate([cb_ckv, pr['ckv']], 1), ((0, 0), (0, back), (0, 0)))
    kr_all = jnp.pad(jnp.concatenate([cb_kr, pr['kr']], 1), ((0, 0), (0, back), (0, 0)))
    kb, vb = _kv_expand(ckv_all, kr_all, lw, dw)
    bias = _band_bias(lw['a_rel_bias'], 1)
    ak_all = jnp.concatenate([_head_major(ca_k).astype(BF16), pr['ak']], 2)
    av_all = jnp.concatenate([_head_major(ca_v).astype(BF16), pr['av']], 2)
    o_a = _band_attn(pr['aq'], ak_all, av_all, bias, 1, False)
    o_b = _softmax_attn(pr['bq'], kb, vb, hb=HEADS, tq=t, tsup=n_past, dw=dw, q0=n_past, causal=True)
    ck_all = _pad_rows(jnp.concatenate([_head_major(cc_k).astype(BF16), pr['ck']], 2), 0, back)
    cv_all = _pad_rows(jnp.concatenate([_head_major(cc_v).astype(BF16), pr['cv']], 2), 0, back)
    o_c = _stick_attn(pr['cq'], ck_all, cv_all, hb=HEADS, tq=t, tsup=n_past, dw=dw, q0=n_past)
    o_m = _softmax_attn(pr['mq'], _head_major(cm_k).astype(BF16), _augment_v(_head_major(cm_v).astype(BF16)),
                        hb=HEADS, tq=t, tsup=cm_k.shape[1], causal=False)
    y = _merge(x, o_a, o_b, o_c, o_m, pr['gates'], lw, t)
    state = (jnp.concatenate([ca_k[:, t:], _token_major(pr['ak_s'])], 1),
             jnp.concatenate([ca_v[:, t:], _token_major(pr['av_s'])], 1),
             pr['ckv'], pr['kr'], _token_major(pr['ck_s']), _token_major(pr['cv_s']))
    return y, state


def kernel(x_prompt, x_sample, mem_prompt, cache_a_k, cache_a_v, cache_b_ckv, cache_b_krope, cache_c_k, cache_c_v,
           cache_mem_k, cache_mem_v, norm_g, w_in, a_qn_g, a_kn_g, a_rel_bias, b_cq_g, b_wq_b, b_ckv_g, b_wkv_b,
           b_qn_g, b_qr_g, b_kn_g, b_kr_g, m_norm_g, w_mem_kv, m_qn_g, m_kn_g, out_g, w_out):
    p = dict(norm_g=norm_g, w_in=w_in, a_qn_g=a_qn_g, a_kn_g=a_kn_g, a_rel_bias=a_rel_bias, b_cq_g=b_cq_g,
             b_wq_b=b_wq_b, b_ckv_g=b_ckv_g, b_wkv_b=b_wkv_b, b_qn_g=b_qn_g, b_qr_g=b_qr_g, b_kn_g=b_kn_g,
             b_kr_g=b_kr_g, m_norm_g=m_norm_g, w_mem_kv=w_mem_kv, m_qn_g=m_qn_g, m_kn_g=m_kn_g, out_g=out_g,
             w_out=w_out)
    depth = w_in.shape[0]
    s, t, n_past = x_prompt.shape[1], x_sample.shape[1], cache_b_ckv.shape[2]
    tabs_p = _rope_tables(jnp.arange(s))
    tabs_s = _rope_tables(n_past + jnp.arange(t))
    hp, hs = x_prompt, x_sample
    p_states, s_states = [], []
    for l in range(depth):
        lw = _layer_weights(l, p)
        hp, sp = _layer_prompt(hp, mem_prompt, lw, tabs_p)
        hs, ss = _layer_step(hs, lw, tabs_s, cache_a_k[l], cache_a_v[l], cache_b_ckv[l], cache_b_krope[l],
                             cache_c_k[l], cache_c_v[l], cache_mem_k[l], cache_mem_v[l])
        p_states.append(sp)
        s_states.append(ss)
    outs_p = [jnp.stack([st[i] for st in p_states]) for i in range(8)]
    outs_s = [jnp.stack([st[i] for st in s_states]) for i in range(6)]
    return (hp, hs, *outs_p, *outs_s)
```

```python
import functools

import jax
import jax.numpy as jnp
from jax import lax
from jax.experimental import pallas as pl
from jax.experimental.pallas import tpu as pltpu

F32 = jnp.float32
BF16 = jnp.bfloat16

D_MODEL = 1024
HEAD_DIM = 64
HEADS = 4
GROUP = HEADS * HEAD_DIM
CHUNK = 64
A_PREV_CHUNKS = 8
A_WIN = A_PREV_CHUNKS * CHUNK
REL_CLIP = 256
B_NOPE = 64
B_ROPE = 32
B_QK = B_NOPE + B_ROPE
Q_LORA = 256
KV_LORA = 128
ROPE_THETA = 10000.0
MLA_SCALE = B_QK ** -0.5
HEAD_SCALE = HEAD_DIM ** -0.5
EPS = 1e-6
NEG = -1e30
LOG2E = 1.4426950408889634
SIGN_BIT = -2 ** 31
SUB = 256
PROMPT_HB = 4
LANES = 128
VMEM_LIMIT = 48 * 1024 * 1024

OFF_AQ, OFF_AK, OFF_AV, OFF_AG = 0, 256, 512, 768
OFF_BCQ, OFF_BCKV, OFF_BKR, OFF_BG = 1024, 1280, 1408, 1536
OFF_CQ, OFF_CK, OFF_CV, OFF_CG = 1792, 2048, 2304, 2560
OFF_MQ, OFF_MG = 2816, 3072
IN_WIDTH_PADDED = 3328
ROPE_LANE0 = B_NOPE


def _params(n_axes):
    return pltpu.CompilerParams(dimension_semantics=("arbitrary",) * n_axes, vmem_limit_bytes=VMEM_LIMIT)


def _dot(a, b):
    return jnp.dot(a, b, preferred_element_type=F32)


def _dot_nt(a, b):
    return lax.dot_general(a, b, (((1,), (1,)), ((), ())), preferred_element_type=F32)


def _group_mean(x2, g):
    return _dot(x2.astype(BF16), g)


def _rope(x, c, s1, s2):
    return x * c + pltpu.roll(x, LANES - B_ROPE // 2, 1) * s1 + pltpu.roll(x, B_ROPE // 2, 1) * s2


def _proj_kernel(x_ref, c_ref, s1_ref, s2_ref, ng_ref, win_ref, g64_ref, aqg_ref, akg_ref, bcqg_ref, wq_ref,
                 gq_ref, gqg_ref, ckvg_ref, krg_ref, mqg_ref,
                 aq_o, ak_o, av_o, aks_o, avs_o, bq_o, ckv_o, kr_o, cq_o, ck_o, cv_o, cks_o, cvs_o, mq_o, gate_o,
                 scr):
    x = x_ref[0]
    ms = jnp.mean(x * x, axis=-1, keepdims=True)
    xn = (x * lax.rsqrt(ms + EPS) * ng_ref[...]).astype(BF16)
    c, s1, s2 = c_ref[...], s1_ref[...], s2_ref[...]

    def seg(off, width):
        return _dot(xn, win_ref[:, off:off + width])

    def head_rms(h, gain_ref):
        m = _group_mean(h * h, g64_ref[...])
        return h * lax.rsqrt(m + EPS) * gain_ref[...]

    def put_heads(o_ref, val, base=0):
        scr[...] = val
        for h in range(HEADS):
            o_ref[0, base + h] = scr[:, h * HEAD_DIM:(h + 1) * HEAD_DIM].astype(o_ref.dtype)

    put_heads(aq_o, head_rms(seg(OFF_AQ, GROUP), aqg_ref) * HEAD_SCALE)
    ak = head_rms(seg(OFF_AK, GROUP), akg_ref)
    aks_o[0] = ak
    put_heads(ak_o, ak)
    av = seg(OFF_AV, GROUP)
    avs_o[0] = av
    put_heads(av_o, av)

    for gi, off in enumerate((OFF_AG, OFF_BG, OFF_CG, OFF_MG)):
        g = seg(off, GROUP)
        put_heads(gate_o, g * jax.nn.sigmoid(g), base=HEADS * gi)

    h = seg(OFF_BCQ, Q_LORA)
    cq = (h * lax.rsqrt(jnp.mean(h * h, axis=-1, keepdims=True) + EPS) * bcqg_ref[...]).astype(BF16)
    qb = _dot(cq, wq_ref[...])
    for half in range(2):
        part = qb[:, half * 2 * LANES:(half + 1) * 2 * LANES]
        m = _group_mean(part * part, gq_ref[...])
        normed = part * lax.rsqrt(m + EPS) * gqg_ref[:, half * 2 * LANES:(half + 1) * 2 * LANES]
        for hh in range(2):
            xh = normed[:, hh * LANES:(hh + 1) * LANES]
            bq_o[0, 2 * half + hh] = (_rope(xh, c, s1, s2) * (MLA_SCALE * LOG2E)).astype(BF16)

    h = seg(OFF_BCKV, KV_LORA)
    ckv_o[0] = h * lax.rsqrt(jnp.mean(h * h, axis=-1, keepdims=True) + EPS) * ckvg_ref[...]
    h = seg(OFF_BKR, LANES)
    ms = jnp.sum(h * h, axis=-1, keepdims=True) * (1.0 / B_ROPE)
    kr = _rope(h * lax.rsqrt(ms + EPS) * krg_ref[...], c, s1, s2)
    scr[:, 0:LANES] = kr
    kr_o[0] = scr[:, ROPE_LANE0:ROPE_LANE0 + B_ROPE]

    put_heads(cq_o, seg(OFF_CQ, GROUP) * (-HEAD_SCALE * LOG2E))
    ck = seg(OFF_CK, GROUP)
    cks_o[0] = ck
    put_heads(ck_o, ck)
    cv = seg(OFF_CV, GROUP)
    cvs_o[0] = cv
    put_heads(cv_o, cv)

    put_heads(mq_o, head_rms(seg(OFF_MQ, GROUP), mqg_ref) * (HEAD_SCALE * LOG2E))


def _proj(x, rope_tabs, lw, tm):
    n, t, _ = x.shape
    nt = t // tm
    c, s1, s2 = rope_tabs

    def full(a):
        return pl.BlockSpec(a.shape, lambda i, j: (0,) * a.ndim)

    tab_spec = pl.BlockSpec((tm, LANES), lambda i, j: (j, 0))
    weights = (lw['norm_g'], lw['w_in'], lw['g64'], lw['a_qn_g'], lw['a_kn_g'], lw['b_cq_g'], lw['wq'], lw['gq'],
               lw['gq_gain'], lw['b_ckv_g'], lw['kr_gain'], lw['m_qn_g'])

    def hm(width, dtype, heads=HEADS):
        return (jax.ShapeDtypeStruct((n, heads, t, width), dtype),
                pl.BlockSpec((1, heads, tm, width), lambda i, j: (i, 0, j, 0)))

    def tmaj(width):
        return (jax.ShapeDtypeStruct((n, t, width), F32), pl.BlockSpec((1, tm, width), lambda i, j: (i, j, 0)))

    outs = [hm(HEAD_DIM, BF16), hm(HEAD_DIM, BF16), hm(HEAD_DIM, BF16), tmaj(GROUP), tmaj(GROUP),
            hm(LANES, BF16), tmaj(KV_LORA), tmaj(B_ROPE),
            hm(HEAD_DIM, BF16), hm(HEAD_DIM, BF16), hm(HEAD_DIM, BF16), tmaj(GROUP), tmaj(GROUP),
            hm(HEAD_DIM, BF16), hm(HEAD_DIM, F32, 4 * HEADS)]
    names = ('aq', 'ak', 'av', 'ak_s', 'av_s', 'bq', 'ckv', 'kr', 'cq', 'ck', 'cv', 'ck_s', 'cv_s', 'mq', 'gates')
    res = pl.pallas_call(
        _proj_kernel,
        grid=(n, nt),
        in_specs=[pl.BlockSpec((1, tm, D_MODEL), lambda i, j: (i, j, 0)), tab_spec, tab_spec, tab_spec]
        + [full(w) for w in weights],
        out_specs=[o[1] for o in outs],
        out_shape=[o[0] for o in outs],
        scratch_shapes=[pltpu.VMEM((tm, GROUP), F32)],
        compiler_params=_params(2),
        name='proj',
    )(x, c, s1, s2, *weights)
    return dict(zip(names, res))


def _kv_kernel(ckv_ref, kr_ref, wkv_ref, place_ref, gk_ref, gkg_ref, vone_ref, k_o, v_o):
    kv = _dot(ckv_ref[0].astype(BF16), wkv_ref[...])
    kr = _dot(kr_ref[0].astype(BF16), place_ref[...])
    for half in range(2):
        part = kv[:, half * 2 * LANES:(half + 1) * 2 * LANES]
        m = _group_mean(part * part, gk_ref[...])
        normed = part * lax.rsqrt(m + EPS) * gkg_ref[:, half * 2 * LANES:(half + 1) * 2 * LANES]
        for hh in range(2):
            k_o[0, 2 * half + hh] = (normed[:, hh * LANES:(hh + 1) * LANES] + kr).astype(BF16)
    for h in range(HEADS):
        v_o[0, h] = (kv[:, (HEADS + h) * LANES:(HEADS + h + 1) * LANES] + vone_ref[...]).astype(BF16)


def _kv_expand(ckv, kr, lw, tl):
    n, l, _ = ckv.shape
    weights = (lw['wkv'], lw['place'], lw['gk'], lw['gk_gain'], lw['v_one'])
    return pl.pallas_call(
        _kv_kernel,
        grid=(n, l // tl),
        in_specs=[pl.BlockSpec((1, tl, KV_LORA), lambda i, j: (i, j, 0)),
                  pl.BlockSpec((1, tl, B_ROPE), lambda i, j: (i, j, 0))]
        + [pl.BlockSpec(w.shape, lambda i, j: (0, 0)) for w in weights],
        out_specs=[pl.BlockSpec((1, HEADS, tl, LANES), lambda i, j: (i, 0, j, 0)),
                   pl.BlockSpec((1, HEADS, tl, LANES), lambda i, j: (i, 0, j, 0))],
        out_shape=[jax.ShapeDtypeStruct((n, HEADS, l, LANES), BF16)] * 2,
        compiler_params=_params(2),
        name='kv_expand',
    )(ckv, kr, *weights)


def _memkv_kernel(mem_ref, ng_ref, w_ref, g64_ref, kg_ref, k_o, v_o):
    x = mem_ref[0]
    xn = (x * lax.rsqrt(jnp.mean(x * x, axis=-1, keepdims=True) + EPS) * ng_ref[...]).astype(BF16)
    kv = _dot(xn, w_ref[...])
    k = kv[:, :GROUP]
    m = _group_mean(k * k, g64_ref[...])
    k_o[0] = k * lax.rsqrt(m + EPS) * kg_ref[...]
    v_o[0] = kv[:, GROUP:]


def _mem_kv(mem, lw):
    n, nm, _ = mem.shape
    weights = (lw['m_norm_g'], lw['w_mem_kv'], lw['g64'], lw['m_kn_g'])
    return pl.pallas_call(
        _memkv_kernel,
        grid=(n,),
        in_specs=[pl.BlockSpec((1, nm, D_MODEL), lambda i: (i, 0, 0))]
        + [pl.BlockSpec(w.shape, lambda i: (0, 0)) for w in weights],
        out_specs=[pl.BlockSpec((1, nm, GROUP), lambda i: (i, 0, 0))] * 2,
        out_shape=[jax.ShapeDtypeStruct((n, nm, GROUP), F32)] * 2,
        compiler_params=_params(1),
        name='mem_kv',
    )(mem, *weights)


def _band_kernel(q_ref, k_ref, v_ref, bias_ref, o_ref, *, rows, span, front):
    step = pl.program_id(2)
    start = pl.multiple_of(step * rows, CHUNK)

    def attend(first_key, width):
        k = k_ref[0, 0, pl.ds(first_key, width), :]
        v = v_ref[0, 0, pl.ds(first_key, width), :]
        s = _dot_nt(q_ref[0, 0], k) + bias_ref[0, :, span - width:]
        p = jnp.exp(s - jnp.max(s, axis=-1, keepdims=True))
        l = jnp.sum(p, axis=-1, keepdims=True)
        o_ref[0, 0] = _dot(p.astype(BF16), v) / l

    early = (A_WIN - front) // rows
    for e in range(early):
        pl.when(step == e)(functools.partial(attend, 0, front + (e + 1) * rows))
    pl.when(step >= early)(lambda: attend(pl.multiple_of(start + front - A_WIN, CHUNK), span))


def _band_attn(q, k, v, bias, g, front):
    n, _, t, _ = q.shape
    rows, span = CHUNK * g, CHUNK * (g + A_PREV_CHUNKS)
    lk = k.shape[2]
    assert (A_WIN - front) % rows == 0 and lk == front + t
    return pl.pallas_call(
        functools.partial(_band_kernel, rows=rows, span=span, front=front),
        grid=(n, HEADS, t // rows),
        in_specs=[pl.BlockSpec((1, 1, rows, HEAD_DIM), lambda i, h, j: (i, h, j, 0)),
                  pl.BlockSpec((1, 1, lk, HEAD_DIM), lambda i, h, j: (i, h, 0, 0)),
                  pl.BlockSpec((1, 1, lk, HEAD_DIM), lambda i, h, j: (i, h, 0, 0)),
                  pl.BlockSpec((1, rows, span), lambda i, h, j: (h, 0, 0))],
        out_specs=pl.BlockSpec((1, 1, rows, HEAD_DIM), lambda i, h, j: (i, h, j, 0)),
        out_shape=jax.ShapeDtypeStruct((n, HEADS, t, HEAD_DIM), F32),
        compiler_params=_params(3),
        name='band_attn',
    )(q, k, v, bias)


def _band_bias(tab, g):
    rows, span = CHUNK * g, CHUNK * (g + A_PREV_CHUNKS)
    i = jnp.arange(rows)[:, None]
    j = jnp.arange(span)[None, :]
    lo = (i // CHUNK) * CHUNK
    inband = (j >= lo) & (j < lo + CHUNK * (A_PREV_CHUNKS + 1))
    period = rows + span - 1
    u = jnp.concatenate([jnp.arange(span), jnp.arange(-(rows - 1), 0)])
    line = tab[:, jnp.clip(A_WIN - u, -REL_CLIP, REL_CLIP) + REL_CLIP].astype(F32)
    skew = jnp.tile(line, (1, rows))[:, :rows * (period - 1)].reshape(-1, rows, period - 1)
    return jnp.where(inband[None], skew[:, :, :span], NEG)


def _resident_spec(block, nq):
    mode = dict(pipeline_mode=pl.Buffered(1)) if nq > 1 else {}
    return pl.BlockSpec(block, lambda i, h, j: (i, h, 0, 0), **mode)


def _softmax_kernel(q_ref, k_ref, v_ref, o_ref, *, hb, tq, tsup, dw, q0, causal, lk):
    q_start = pl.multiple_of(q0 + pl.program_id(2) * tq, CHUNK)

    def score_stage(h, off, n, width):
        q = q_ref[0, h]
        return tuple(_dot_nt(q, k_ref[0, h, pl.ds(pl.multiple_of(off + j * width, CHUNK), width), :])
                     for j in range(n))

    def chunk_mask(scores, width):
        qchunk = lax.broadcasted_iota(jnp.int32, (tq, width), 0) // CHUNK
        return tuple(
            jnp.where((j * width + lax.broadcasted_iota(jnp.int32, (tq, width), 1)) // CHUNK <= qchunk, s, NEG)
            for j, s in enumerate(scores))

    def value_stage(h, off, width, scores, m, acc):
        m_new = m
        for s in scores:
            m_new = jnp.maximum(m_new, jnp.max(s, axis=-1, keepdims=True))
        pv = None
        for j, s in enumerate(scores):
            v = v_ref[0, h, pl.ds(pl.multiple_of(off + j * width, CHUNK), width), :]
            d = _dot(jnp.exp2(s - m_new).astype(BF16), v)
            pv = d if pv is None else pv + d
        return m_new, jnp.exp2(m - m_new) * acc + pv

    heads = range(hb)
    sub = min(tsup, SUB)
    ms = tuple(jnp.full((tq, 1), NEG, F32) for _ in heads)
    accs = tuple(jnp.zeros((tq, LANES), F32) for _ in heads)
    def sweep(i, carry):
        ms, accs = carry
        off = pl.multiple_of(i * tsup, tsup)
        out = [value_stage(h, off, sub, score_stage(h, off, tsup // sub, sub), ms[h], accs[h]) for h in heads]
        return tuple(o[0] for o in out), tuple(o[1] for o in out)

    ms, accs = lax.fori_loop(0, q_start // tsup if causal else lk // tsup, sweep, (ms, accs))
    if causal:
        dsub = min(dw, SUB)
        accs = [value_stage(h, q_start, dsub, chunk_mask(score_stage(h, q_start, dw // dsub, dsub), dsub),
                            ms[h], accs[h])[1] for h in heads]
    for h in heads:
        acc = accs[h]
        o_ref[0, h] = (acc / acc[:, HEAD_DIM:HEAD_DIM + 1])[:, :HEAD_DIM]


def _softmax_attn(q, k, v, *, hb, tq, tsup, dw=0, q0=0, causal):
    n, _, t, dk = q.shape
    lk = k.shape[2]
    nq = t // tq
    assert t % tq == 0 and tq % CHUNK == 0 and q0 % CHUNK == 0 and HEADS % hb == 0
    if causal:
        assert q0 % tsup == 0 and (nq == 1 or tq % tsup == 0) and q0 + (nq - 1) * tq + dw <= lk
    else:
        assert lk % tsup == 0
    return pl.pallas_call(
        functools.partial(_softmax_kernel, hb=hb, tq=tq, tsup=tsup, dw=dw, q0=q0, causal=causal, lk=lk),
        grid=(n, HEADS // hb, nq),
        in_specs=[pl.BlockSpec((1, hb, tq, dk), lambda i, h, j: (i, h, j, 0)),
                  _resident_spec((1, hb, lk, dk), nq), _resident_spec((1, hb, lk, LANES), nq)],
        out_specs=pl.BlockSpec((1, hb, tq, HEAD_DIM), lambda i, h, j: (i, h, j, 0)),
        out_shape=jax.ShapeDtypeStruct((n, HEADS, t, HEAD_DIM), F32),
        compiler_params=_params(3),
        name='softmax_attn',
    )(q, k, v)


def _augment_v(v):
    n, h, l, _ = v.shape
    return jnp.concatenate([v, jnp.ones((n, h, l, 1), v.dtype), jnp.zeros((n, h, l, LANES - HEAD_DIM - 1), v.dtype)],
                           axis=-1)


def _stick_kernel(q_ref, k_ref, v_ref, tri_ref, o_ref, *, hb, tq, tsup, dw, q0):
    q_start = pl.multiple_of(q0 + pl.program_id(2) * tq, CHUNK)

    def score_stage(h, off, n, width, masked):
        tri = tri_ref[...] if width == SUB else tri_ref[0:width, 0:width]
        q = q_ref[0, h]
        parts = []
        for j in range(n):
            k = k_ref[0, h, pl.ds(pl.multiple_of(off + j * width, CHUNK), width), :]
            nz = _dot_nt(q, k)
            neg_abs = lax.bitcast_convert_type(lax.bitcast_convert_type(nz, jnp.int32) | SIGN_BIT, F32)
            log_1m = jnp.minimum(nz, 0.0) - jnp.log(1.0 + jnp.exp2(neg_abs)) * LOG2E
            if masked:
                mask = (j * width + lax.broadcasted_iota(jnp.int32, (tq, width), 1)
                        < lax.broadcasted_iota(jnp.int32, (tq, width), 0))
                log_1m = jnp.where(mask, log_1m, 0.0)
            tail = _dot(log_1m.astype(BF16), tri)
            expo = log_1m - nz + tail
            if masked:
                expo = jnp.where(mask, expo, NEG)
            parts.append((expo, tail[:, 0:1] + log_1m[:, 0:1]))
        return tuple(parts)

    def value_stage(h, off, width, parts, run, acc):
        for j in reversed(range(len(parts))):
            expo, row_sum = parts[j]
            v = v_ref[0, h, pl.ds(pl.multiple_of(off + j * width, CHUNK), width), :]
            acc = acc + _dot(jnp.exp2(expo).astype(BF16), v) * jnp.exp2(run)
            run = run + row_sum
        return run, acc

    heads = range(hb)
    dsub = min(dw, SUB)
    runs = tuple(jnp.zeros((tq, 1), F32) for _ in heads)
    accs = tuple(jnp.zeros((tq, HEAD_DIM), F32) for _ in heads)
    diag = tuple(score_stage(h, q_start, dw // dsub, dsub, True) for h in heads)
    n_super = q_start // tsup

    out = [value_stage(h, q_start, dsub, diag[h], runs[h], accs[h]) for h in heads]
    runs, accs = tuple(o[0] for o in out), tuple(o[1] for o in out)

    def sweep(i, carry):
        runs, accs = carry
        off = pl.multiple_of((n_super - 1 - i) * tsup, tsup)
        out = [value_stage(h, off, SUB, score_stage(h, off, tsup // SUB, SUB, False), runs[h], accs[h])
               for h in heads]
        return tuple(o[0] for o in out), tuple(o[1] for o in out)

    _, accs = lax.fori_loop(0, n_super, sweep, (runs, accs))
    for h in heads:
        o_ref[0, h] = accs[h]


def _stick_attn(q, k, v, *, hb, tq, tsup, dw, q0=0):
    n, _, t, _ = q.shape
    lk = k.shape[2]
    nq = t // tq
    assert t % tq == 0 and q0 % CHUNK == 0 and tq % CHUNK == 0 and tsup % SUB == 0 and HEADS % hb == 0
    assert q0 % tsup == 0 and (nq == 1 or tq % tsup == 0) and q0 + (nq - 1) * tq + dw <= lk
    r = jnp.arange(SUB)
    tri = (r[:, None] > r[None, :]).astype(BF16)
    return pl.pallas_call(
        functools.partial(_stick_kernel, hb=hb, tq=tq, tsup=tsup, dw=dw, q0=q0),
        grid=(n, HEADS // hb, nq),
        in_specs=[pl.BlockSpec((1, hb, tq, HEAD_DIM), lambda i, h, j: (i, h, j, 0)),
                  _resident_spec((1, hb, lk, HEAD_DIM), nq), _resident_spec((1, hb, lk, HEAD_DIM), nq),
                  pl.BlockSpec((SUB, SUB), lambda i, h, j: (0, 0))],
        out_specs=pl.BlockSpec((1, hb, tq, HEAD_DIM), lambda i, h, j: (i, h, j, 0)),
        out_shape=jax.ShapeDtypeStruct((n, HEADS, t, HEAD_DIM), F32),
        compiler_params=_params(3),
        name='stick_attn',
    )(q, k, v, tri)


def _merge_kernel(x_ref, oa_ref, ob_ref, oc_ref, om_ref, gate_ref, og_ref, w_ref, y_ref):
    outs = [o_ref[0, h] for o_ref in (oa_ref, ob_ref, oc_ref, om_ref) for h in range(HEADS)]
    inv = [lax.rsqrt(jnp.mean(o * o, axis=-1, keepdims=True) + EPS) for o in outs]
    ys = [o * r * og_ref[j] * gate_ref[0, j] for j, (o, r) in enumerate(zip(outs, inv))]
    acc = x_ref[0]
    for pair in range(2 * HEADS):
        y = jnp.concatenate(ys[2 * pair:2 * pair + 2], axis=-1).astype(BF16)
        acc = acc + _dot(y, w_ref[pair])
    y_ref[0] = acc


def _merge(x, o_a, o_b, o_c, o_m, gates, lw, tm):
    n, t, _ = x.shape
    hspec = pl.BlockSpec((1, HEADS, tm, HEAD_DIM), lambda i, j: (i, 0, j, 0))
    return pl.pallas_call(
        _merge_kernel,
        grid=(n, t // tm),
        in_specs=[pl.BlockSpec((1, tm, D_MODEL), lambda i, j: (i, j, 0)), hspec, hspec, hspec, hspec,
                  pl.BlockSpec((1, 4 * HEADS, tm, HEAD_DIM), lambda i, j: (i, 0, j, 0)),
                  pl.BlockSpec((4 * HEADS, 1, HEAD_DIM), lambda i, j: (0, 0, 0)),
                  pl.BlockSpec((2 * HEADS, 2 * HEAD_DIM, D_MODEL), lambda i, j: (0, 0, 0))],
        out_specs=pl.BlockSpec((1, tm, D_MODEL), lambda i, j: (i, j, 0)),
        out_shape=jax.ShapeDtypeStruct((n, t, D_MODEL), F32),
        compiler_params=_params(2),
        name='merge',
    )(x, o_a, o_b, o_c, o_m, gates, lw['out_g'], lw['w_out'])


def _block_diag(block, reps):
    return jnp.kron(jnp.eye(reps, dtype=F32), block)


def _layer_weights(l, p):
    w_in = p['w_in'][l]
    zeros = lambda c: jnp.zeros((D_MODEL, c), F32)
    kr0 = 1408
    w_in = jnp.concatenate([w_in[:, :kr0], zeros(ROPE_LANE0), w_in[:, kr0:kr0 + B_ROPE],
                            zeros(LANES - ROPE_LANE0 - B_ROPE), w_in[:, kr0 + B_ROPE:]], axis=1)
    assert w_in.shape[1] == IN_WIDTH_PADDED
    ones = lambda k: jnp.full((k, k), 1.0 / k, F32)
    zpad = lambda k: jnp.zeros((k, k), F32)
    tile = lambda g, reps=HEADS: jnp.tile(g, reps)[None, :]
    pad_to = lambda g, k: jnp.concatenate([g, jnp.zeros((k - g.shape[0],), F32)])
    wq = jnp.pad(p['b_wq_b'][l].reshape(Q_LORA, HEADS, B_QK), ((0, 0), (0, 0), (0, LANES - B_QK)))
    wkv = p['b_wkv_b'][l].reshape(KV_LORA, HEADS, B_NOPE + HEAD_DIM)
    wk = jnp.pad(wkv[:, :, :B_NOPE], ((0, 0), (0, 0), (0, LANES - B_NOPE))).reshape(KV_LORA, HEADS * LANES)
    wv = jnp.pad(wkv[:, :, B_NOPE:], ((0, 0), (0, 0), (0, LANES - HEAD_DIM))).reshape(KV_LORA, HEADS * LANES)
    gq128 = jax.scipy.linalg.block_diag(ones(B_NOPE), ones(B_ROPE), zpad(LANES - B_QK))
    gk128 = jax.scipy.linalg.block_diag(ones(B_NOPE), zpad(LANES - B_NOPE))
    place = jnp.zeros((B_ROPE, LANES), F32).at[jnp.arange(B_ROPE), ROPE_LANE0 + jnp.arange(B_ROPE)].set(1.0)
    return dict(
        norm_g=p['norm_g'][l][None, :],
        w_in=w_in.astype(BF16),
        g64=_block_diag(ones(HEAD_DIM), HEADS).astype(BF16),
        a_qn_g=tile(p['a_qn_g'][l]), a_kn_g=tile(p['a_kn_g'][l]),
        b_cq_g=p['b_cq_g'][l][None, :],
        wq=wq.reshape(Q_LORA, HEADS * LANES).astype(BF16),
        gq=_block_diag(gq128, 2).astype(BF16),
        gq_gain=tile(jnp.concatenate([p['b_qn_g'][l], pad_to(p['b_qr_g'][l], LANES - B_NOPE)])),
        b_ckv_g=p['b_ckv_g'][l][None, :],
        kr_gain=jnp.concatenate([jnp.zeros((ROPE_LANE0,), F32), pad_to(p['b_kr_g'][l], LANES - ROPE_LANE0)])[None, :],
        m_qn_g=tile(p['m_qn_g'][l]),
        wkv=jnp.concatenate([wk, wv], axis=1).astype(BF16),
        place=place.astype(BF16),
        gk=_block_diag(gk128, 2).astype(BF16),
        gk_gain=tile(pad_to(p['b_kn_g'][l], LANES)),
        v_one=jnp.zeros((1, LANES), F32).at[0, HEAD_DIM].set(1.0),
        m_norm_g=p['m_norm_g'][l][None, :],
        w_mem_kv=p['w_mem_kv'][l].astype(BF16),
        m_kn_g=tile(p['m_kn_g'][l]),
        out_g=p['out_g'][l].reshape(4 * HEADS, 1, HEAD_DIM),
        w_out=p['w_out'][l].reshape(2 * HEADS, 2 * HEAD_DIM, D_MODEL).astype(BF16),
        a_rel_bias=p['a_rel_bias'][l],
    )


def _rope_tables(pos):
    half = B_ROPE // 2
    freqs = ROPE_THETA ** (-jnp.arange(half, dtype=F32) / half)
    ang = pos.astype(F32)[:, None] * freqs[None, :]
    cos, sin = jnp.cos(ang), jnp.sin(ang)
    t = pos.shape[0]
    one, zero = jnp.ones((t, ROPE_LANE0), F32), jnp.zeros((t, ROPE_LANE0), F32)
    tail, z16 = jnp.zeros((t, LANES - ROPE_LANE0 - B_ROPE), F32), jnp.zeros((t, half), F32)
    c = jnp.concatenate([one, cos, cos, tail], axis=1)
    s1 = jnp.concatenate([zero, -sin, z16, tail], axis=1)
    s2 = jnp.concatenate([zero, z16, sin, tail], axis=1)
    return c, s1, s2


def _head_major(a):
    return a.transpose(0, 2, 1, 3)


def _token_major(a):
    return a.reshape(a.shape[0], a.shape[1], HEADS, HEAD_DIM)


def _pad_rows(a, front, back):
    return jnp.pad(a, ((0, 0), (0, 0), (front, back), (0, 0)))


def _tiles(s):
    tm = min(256, s)
    tq = min(512, s)
    g = min(4, s // CHUNK)
    return tm, tq, g


def _layer_prompt(x, mem, lw, rope_tabs):
    n, s, _ = x.shape
    tm, tq, g = _tiles(s)
    pr = _proj(x, rope_tabs, lw, tm)
    kb, vb = _kv_expand(pr['ckv'], pr['kr'], lw, min(1024, s))
    bias = _band_bias(lw['a_rel_bias'], g)
    o_a = _band_attn(pr['aq'], pr['ak'], pr['av'], bias, g, 0)
    o_b = _softmax_attn(pr['bq'], kb, vb, hb=PROMPT_HB, tq=tq, tsup=tq, dw=tq, causal=True)
    o_c = _stick_attn(pr['cq'], pr['ck'], pr['cv'], hb=PROMPT_HB, tq=tq, tsup=tq, dw=tq)
    mk, mv = _mem_kv(mem, lw)
    mk4, mv4 = _token_major(mk), _token_major(mv)
    o_m = _softmax_attn(pr['mq'], _head_major(mk4).astype(BF16), _augment_v(_head_major(mv4).astype(BF16)),
                        hb=1, tq=tq, tsup=mk.shape[1], causal=False)
    y = _merge(x, o_a, o_b, o_c, o_m, pr['gates'], lw, tm)
    keep = min(A_WIN, s)
    state = (_token_major(pr['ak_s'][:, s - keep:]), _token_major(pr['av_s'][:, s - keep:]), pr['ckv'], pr['kr'],
             _token_major(pr['ck_s']), _token_major(pr['cv_s']), mk4, mv4)
    return y, state


def _layer_step(x, lw, rope_tabs, ca_k, ca_v, cb_ckv, cb_kr, cc_k, cc_v, cm_k, cm_v):
    n, t, _ = x.shape
    n_past = cb_ckv.shape[1]
    assert t == CHUNK and n_past % CHUNK == 0 and ca_k.shape[1] == A_WIN
    dw = 2 * CHUNK
    back = dw - t
    pr = _proj(x, rope_tabs, lw, t)
    ckv_all = jnp.pad(jnp.concatenate([cb_ckv, pr['ckv']], 1), ((0, 0), (0, back), (0, 0)))
    kr_all = jnp.pad(jnp.concatenate([cb_kr, pr['kr']], 1), ((0, 0), (0, back), (0, 0)))
    kb, vb = _kv_expand(ckv_all, kr_all, lw, ckv_all.shape[1])
    bias = _band_bias(lw['a_rel_bias'], 1)
    ak_all = jnp.concatenate([_head_major(ca_k).astype(BF16), pr['ak']], 2)
    av_all = jnp.concatenate([_head_major(ca_v).astype(BF16), pr['av']], 2)
    o_a = _band_attn(pr['aq'], ak_all, av_all, bias, 1, A_WIN)
    o_b = _softmax_attn(pr['bq'], kb, vb, hb=HEADS, tq=t, tsup=n_past, dw=dw, q0=n_past, causal=True)
    ck_all = _pad_rows(jnp.concatenate([_head_major(cc_k).astype(BF16), pr['ck']], 2), 0, back)
    cv_all = _pad_rows(jnp.concatenate([_head_major(cc_v).astype(BF16), pr['cv']], 2), 0, back)
    o_c = _stick_attn(pr['cq'], ck_all, cv_all, hb=HEADS, tq=t, tsup=n_past, dw=dw, q0=n_past)
    o_m = _softmax_attn(pr['mq'], _head_major(cm_k).astype(BF16), _augment_v(_head_major(cm_v).astype(BF16)),
                        hb=HEADS, tq=t, tsup=cm_k.shape[1], causal=False)
    y = _merge(x, o_a, o_b, o_c, o_m, pr['gates'], lw, t)
    state = (jnp.concatenate([ca_k[:, t:], _token_major(pr['ak_s'])], 1),
             jnp.concatenate([ca_v[:, t:], _token_major(pr['av_s'])], 1),
             pr['ckv'], pr['kr'], _token_major(pr['ck_s']), _token_major(pr['cv_s']))
    return y, state


def kernel(x_prompt, x_sample, mem_prompt, cache_a_k, cache_a_v, cache_b_ckv, cache_b_krope, cache_c_k, cache_c_v,
           cache_mem_k, cache_mem_v, norm_g, w_in, a_qn_g, a_kn_g, a_rel_bias, b_cq_g, b_wq_b, b_ckv_g, b_wkv_b,
           b_qn_g, b_qr_g, b_kn_g, b_kr_g, m_norm_g, w_mem_kv, m_qn_g, m_kn_g, out_g, w_out):
    p = dict(norm_g=norm_g, w_in=w_in, a_qn_g=a_qn_g, a_kn_g=a_kn_g, a_rel_bias=a_rel_bias, b_cq_g=b_cq_g,
             b_wq_b=b_wq_b, b_ckv_g=b_ckv_g, b_wkv_b=b_wkv_b, b_qn_g=b_qn_g, b_qr_g=b_qr_g, b_kn_g=b_kn_g,
             b_kr_g=b_kr_g, m_norm_g=m_norm_g, w_mem_kv=w_mem_kv, m_qn_g=m_qn_g, m_kn_g=m_kn_g, out_g=out_g,
             w_out=w_out)
    depth = w_in.shape[0]
    s, t, n_past = x_prompt.shape[1], x_sample.shape[1], cache_b_ckv.shape[2]
    tabs_p = _rope_tables(jnp.arange(s))
    tabs_s = _rope_tables(n_past + jnp.arange(t))
    hp, hs = x_prompt, x_sample
    p_states, s_states = [], []
    for l in range(depth):
        lw = _layer_weights(l, p)
        hp, sp = _layer_prompt(hp, mem_prompt, lw, tabs_p)
        hs, ss = _layer_step(hs, lw, tabs_s, cache_a_k[l], cache_a_v[l], cache_b_ckv[l], cache_b_krope[l],
                             cache_c_k[l], cache_c_v[l], cache_mem_k[l], cache_mem_v[l])
        p_states.append(sp)
        s_states.append(ss)
    outs_p = [jnp.stack([st[i] for st in p_states]) for i in range(8)]
    outs_s = [jnp.stack([st[i] for st in s_states]) for i in range(6)]
    return (hp, hs, *outs_p, *outs_s)
```

```python
import functools

import jax
import jax.numpy as jnp
from jax import lax
from jax.experimental import pallas as pl
from jax.experimental.pallas import tpu as pltpu

F32 = jnp.float32
BF16 = jnp.bfloat16

D_MODEL = 1024
HEAD_DIM = 64
HEADS = 4
GROUP = HEADS * HEAD_DIM
CHUNK = 64
A_PREV_CHUNKS = 8
A_WIN = A_PREV_CHUNKS * CHUNK
REL_CLIP = 256
B_NOPE = 64
B_ROPE = 32
B_QK = B_NOPE + B_ROPE
Q_LORA = 256
KV_LORA = 128
ROPE_THETA = 10000.0
MLA_SCALE = B_QK ** -0.5
HEAD_SCALE = HEAD_DIM ** -0.5
EPS = 1e-6
NEG = -1e30
LOG2E = 1.4426950408889634
SIGN_BIT = -2 ** 31
SUB = 256
PROMPT_TQ = 1024
PROMPT_HB = 2
LANES = 128
VMEM_LIMIT = 48 * 1024 * 1024

OFF_AQ, OFF_AK, OFF_AV, OFF_AG = 0, 256, 512, 768
OFF_BCQ, OFF_BCKV, OFF_BKR, OFF_BG = 1024, 1280, 1408, 1536
OFF_CQ, OFF_CK, OFF_CV, OFF_CG = 1792, 2048, 2304, 2560
OFF_MQ, OFF_MG = 2816, 3072
IN_WIDTH_PADDED = 3328
ROPE_LANE0 = B_NOPE


def _params(n_axes):
    return pltpu.CompilerParams(dimension_semantics=("arbitrary",) * n_axes, vmem_limit_bytes=VMEM_LIMIT)


def _dot(a, b):
    return jnp.dot(a, b, preferred_element_type=F32)


def _dot_nt(a, b):
    return lax.dot_general(a, b, (((1,), (1,)), ((), ())), preferred_element_type=F32)


def _group_mean(x2, g):
    return _dot(x2.astype(BF16), g)


def _rope(x, c, s1, s2):
    return x * c + pltpu.roll(x, LANES - B_ROPE // 2, 1) * s1 + pltpu.roll(x, B_ROPE // 2, 1) * s2


def _proj_kernel(x_ref, c_ref, s1_ref, s2_ref, ng_ref, win_ref, g64_ref, aqg_ref, akg_ref, bcqg_ref, wq_ref,
                 gq_ref, gqg_ref, ckvg_ref, krg_ref, mqg_ref,
                 aq_o, ak_o, av_o, aks_o, avs_o, bq_o, ckv_o, kr_o, cq_o, ck_o, cv_o, cks_o, cvs_o, mq_o, gate_o,
                 scr):
    x = x_ref[0]
    ms = jnp.mean(x * x, axis=-1, keepdims=True)
    xn = (x * lax.rsqrt(ms + EPS) * ng_ref[...]).astype(BF16)
    c, s1, s2 = c_ref[...], s1_ref[...], s2_ref[...]

    def seg(off, width):
        return _dot(xn, win_ref[:, off:off + width])

    def head_rms(h, gain_ref):
        m = _group_mean(h * h, g64_ref[...])
        return h * lax.rsqrt(m + EPS) * gain_ref[...]

    def put_heads(o_ref, val, base=0):
        scr[...] = val
        for h in range(HEADS):
            o_ref[0, base + h] = scr[:, h * HEAD_DIM:(h + 1) * HEAD_DIM].astype(o_ref.dtype)

    put_heads(aq_o, head_rms(seg(OFF_AQ, GROUP), aqg_ref) * HEAD_SCALE)
    ak = head_rms(seg(OFF_AK, GROUP), akg_ref)
    aks_o[0] = ak
    put_heads(ak_o, ak)
    av = seg(OFF_AV, GROUP)
    avs_o[0] = av
    put_heads(av_o, av)

    for gi, off in enumerate((OFF_AG, OFF_BG, OFF_CG, OFF_MG)):
        g = seg(off, GROUP)
        put_heads(gate_o, g * jax.nn.sigmoid(g), base=HEADS * gi)

    h = seg(OFF_BCQ, Q_LORA)
    cq = (h * lax.rsqrt(jnp.mean(h * h, axis=-1, keepdims=True) + EPS) * bcqg_ref[...]).astype(BF16)
    qb = _dot(cq, wq_ref[...])
    for half in range(2):
        part = qb[:, half * 2 * LANES:(half + 1) * 2 * LANES]
        m = _group_mean(part * part, gq_ref[...])
        normed = part * lax.rsqrt(m + EPS) * gqg_ref[:, half * 2 * LANES:(half + 1) * 2 * LANES]
        for hh in range(2):
            xh = normed[:, hh * LANES:(hh + 1) * LANES]
            bq_o[0, 2 * half + hh] = (_rope(xh, c, s1, s2) * (MLA_SCALE * LOG2E)).astype(BF16)

    h = seg(OFF_BCKV, KV_LORA)
    ckv_o[0] = h * lax.rsqrt(jnp.mean(h * h, axis=-1, keepdims=True) + EPS) * ckvg_ref[...]
    h = seg(OFF_BKR, LANES)
    ms = jnp.sum(h * h, axis=-1, keepdims=True) * (1.0 / B_ROPE)
    kr = _rope(h * lax.rsqrt(ms + EPS) * krg_ref[...], c, s1, s2)
    scr[:, 0:LANES] = kr
    kr_o[0] = scr[:, ROPE_LANE0:ROPE_LANE0 + B_ROPE]

    put_heads(cq_o, seg(OFF_CQ, GROUP) * (-HEAD_SCALE * LOG2E))
    ck = seg(OFF_CK, GROUP)
    cks_o[0] = ck
    put_heads(ck_o, ck)
    cv = seg(OFF_CV, GROUP)
    cvs_o[0] = cv
    put_heads(cv_o, cv)

    put_heads(mq_o, head_rms(seg(OFF_MQ, GROUP), mqg_ref) * (HEAD_SCALE * LOG2E))


def _proj(x, rope_tabs, lw, tm):
    n, t, _ = x.shape
    nt = t // tm
    c, s1, s2 = rope_tabs

    def full(a):
        return pl.BlockSpec(a.shape, lambda i, j: (0,) * a.ndim)

    tab_spec = pl.BlockSpec((tm, LANES), lambda i, j: (j, 0))
    weights = (lw['norm_g'], lw['w_in'], lw['g64'], lw['a_qn_g'], lw['a_kn_g'], lw['b_cq_g'], lw['wq'], lw['gq'],
               lw['gq_gain'], lw['b_ckv_g'], lw['kr_gain'], lw['m_qn_g'])

    def hm(width, dtype, heads=HEADS):
        return (jax.ShapeDtypeStruct((n, heads, t, width), dtype),
                pl.BlockSpec((1, heads, tm, width), lambda i, j: (i, 0, j, 0)))

    def tmaj(width):
        return (jax.ShapeDtypeStruct((n, t, width), F32), pl.BlockSpec((1, tm, width), lambda i, j: (i, j, 0)))

    outs = [hm(HEAD_DIM, BF16), hm(HEAD_DIM, BF16), hm(HEAD_DIM, BF16), tmaj(GROUP), tmaj(GROUP),
            hm(LANES, BF16), tmaj(KV_LORA), tmaj(B_ROPE),
            hm(HEAD_DIM, BF16), hm(HEAD_DIM, BF16), hm(HEAD_DIM, BF16), tmaj(GROUP), tmaj(GROUP),
            hm(HEAD_DIM, BF16), hm(HEAD_DIM, F32, 4 * HEADS)]
    names = ('aq', 'ak', 'av', 'ak_s', 'av_s', 'bq', 'ckv', 'kr', 'cq', 'ck', 'cv', 'ck_s', 'cv_s', 'mq', 'gates')
    res = pl.pallas_call(
        _proj_kernel,
        grid=(n, nt),
        in_specs=[pl.BlockSpec((1, tm, D_MODEL), lambda i, j: (i, j, 0)), tab_spec, tab_spec, tab_spec]
        + [full(w) for w in weights],
        out_specs=[o[1] for o in outs],
        out_shape=[o[0] for o in outs],
        scratch_shapes=[pltpu.VMEM((tm, GROUP), F32)],
        compiler_params=_params(2),
        name='proj',
    )(x, c, s1, s2, *weights)
    return dict(zip(names, res))


def _kv_kernel(ckv_ref, kr_ref, wkv_ref, place_ref, gk_ref, gkg_ref, vone_ref, k_o, v_o):
    kv = _dot(ckv_ref[0].astype(BF16), wkv_ref[...])
    kr = _dot(kr_ref[0].astype(BF16), place_ref[...])
    for half in range(2):
        part = kv[:, half * 2 * LANES:(half + 1) * 2 * LANES]
        m = _group_mean(part * part, gk_ref[...])
        normed = part * lax.rsqrt(m + EPS) * gkg_ref[:, half * 2 * LANES:(half + 1) * 2 * LANES]
        for hh in range(2):
            k_o[0, 2 * half + hh] = (normed[:, hh * LANES:(hh + 1) * LANES] + kr).astype(BF16)
    for h in range(HEADS):
        v_o[0, h] = (kv[:, (HEADS + h) * LANES:(HEADS + h + 1) * LANES] + vone_ref[...]).astype(BF16)


def _kv_expand(ckv, kr, lw, tl):
    n, l, _ = ckv.shape
    weights = (lw['wkv'], lw['place'], lw['gk'], lw['gk_gain'], lw['v_one'])
    return pl.pallas_call(
        _kv_kernel,
        grid=(n, l // tl),
        in_specs=[pl.BlockSpec((1, tl, KV_LORA), lambda i, j: (i, j, 0)),
                  pl.BlockSpec((1, tl, B_ROPE), lambda i, j: (i, j, 0))]
        + [pl.BlockSpec(w.shape, lambda i, j: (0, 0)) for w in weights],
        out_specs=[pl.BlockSpec((1, HEADS, tl, LANES), lambda i, j: (i, 0, j, 0)),
                   pl.BlockSpec((1, HEADS, tl, LANES), lambda i, j: (i, 0, j, 0))],
        out_shape=[jax.ShapeDtypeStruct((n, HEADS, l, LANES), BF16)] * 2,
        compiler_params=_params(2),
        name='kv_expand',
    )(ckv, kr, *weights)


def _memkv_kernel(mem_ref, ng_ref, w_ref, g64_ref, kg_ref, k_o, v_o):
    x = mem_ref[0]
    xn = (x * lax.rsqrt(jnp.mean(x * x, axis=-1, keepdims=True) + EPS) * ng_ref[...]).astype(BF16)
    kv = _dot(xn, w_ref[...])
    k = kv[:, :GROUP]
    m = _group_mean(k * k, g64_ref[...])
    k_o[0] = k * lax.rsqrt(m + EPS) * kg_ref[...]
    v_o[0] = kv[:, GROUP:]


def _mem_kv(mem, lw):
    n, nm, _ = mem.shape
    weights = (lw['m_norm_g'], lw['w_mem_kv'], lw['g64'], lw['m_kn_g'])
    return pl.pallas_call(
        _memkv_kernel,
        grid=(n,),
        in_specs=[pl.BlockSpec((1, nm, D_MODEL), lambda i: (i, 0, 0))]
        + [pl.BlockSpec(w.shape, lambda i: (0, 0)) for w in weights],
        out_specs=[pl.BlockSpec((1, nm, GROUP), lambda i: (i, 0, 0))] * 2,
        out_shape=[jax.ShapeDtypeStruct((n, nm, GROUP), F32)] * 2,
        compiler_params=_params(1),
        name='mem_kv',
    )(mem, *weights)


def _band_kernel(q_ref, k_ref, v_ref, bias_ref, o_ref, *, rows, span, front):
    step = pl.program_id(1)
    start = pl.multiple_of(step * rows, CHUNK)

    def attend(first_key, width):
        for h in range(HEADS):
            k = k_ref[0, h, pl.ds(first_key, width), :]
            v = v_ref[0, h, pl.ds(first_key, width), :]
            s = _dot_nt(q_ref[0, h], k) + bias_ref[h, :, span - width:]
            p = jnp.exp(s - jnp.max(s, axis=-1, keepdims=True))
            l = jnp.sum(p, axis=-1, keepdims=True)
            o_ref[0, h] = _dot(p.astype(BF16), v) / l

    early = (A_WIN - front) // rows
    for e in range(early):
        pl.when(step == e)(functools.partial(attend, 0, front + (e + 1) * rows))
    pl.when(step >= early)(lambda: attend(pl.multiple_of(start + front - A_WIN, CHUNK), span))


def _band_attn(q, k, v, bias, g, front):
    n, _, t, _ = q.shape
    rows, span = CHUNK * g, CHUNK * (g + A_PREV_CHUNKS)
    lk = k.shape[2]
    assert (A_WIN - front) % rows == 0 and lk == front + t
    resident = dict(pipeline_mode=pl.Buffered(1)) if t // rows > 1 else {}
    return pl.pallas_call(
        functools.partial(_band_kernel, rows=rows, span=span, front=front),
        grid=(n, t // rows),
        in_specs=[pl.BlockSpec((1, HEADS, rows, HEAD_DIM), lambda i, j: (i, 0, j, 0)),
                  pl.BlockSpec((1, HEADS, lk, HEAD_DIM), lambda i, j: (i, 0, 0, 0), **resident),
                  pl.BlockSpec((1, HEADS, lk, HEAD_DIM), lambda i, j: (i, 0, 0, 0), **resident),
                  pl.BlockSpec((HEADS, rows, span), lambda i, j: (0, 0, 0))],
        out_specs=pl.BlockSpec((1, HEADS, rows, HEAD_DIM), lambda i, j: (i, 0, j, 0)),
        out_shape=jax.ShapeDtypeStruct((n, HEADS, t, HEAD_DIM), F32),
        compiler_params=_params(2),
        name='band_attn',
    )(q, k, v, bias)


def _band_bias(tab, g):
    rows, span = CHUNK * g, CHUNK * (g + A_PREV_CHUNKS)
    i = jnp.arange(rows)[:, None]
    j = jnp.arange(span)[None, :]
    lo = (i // CHUNK) * CHUNK
    inband = (j >= lo) & (j < lo + CHUNK * (A_PREV_CHUNKS + 1))
    period = rows + span - 1
    u = jnp.concatenate([jnp.arange(span), jnp.arange(-(rows - 1), 0)])
    line = tab[:, jnp.clip(A_WIN - u, -REL_CLIP, REL_CLIP) + REL_CLIP].astype(F32)
    skew = jnp.tile(line, (1, rows))[:, :rows * (period - 1)].reshape(-1, rows, period - 1)
    return jnp.where(inband[None], skew[:, :, :span], NEG)


def _resident_spec(block, nq):
    mode = dict(pipeline_mode=pl.Buffered(1)) if nq > 1 else {}
    return pl.BlockSpec(block, lambda i, h, j: (i, h, 0, 0), **mode)


def _softmax_kernel(q_ref, k_ref, v_ref, o_ref, *, hb, tq, tsup, dw, q0, causal, lk):
    q_start = pl.multiple_of(q0 + pl.program_id(2) * tq, CHUNK)

    def score_stage(h, off, n, width):
        q = q_ref[0, h]
        return tuple(_dot_nt(q, k_ref[0, h, pl.ds(pl.multiple_of(off + j * width, CHUNK), width), :])
                     for j in range(n))

    def chunk_mask(scores, width):
        qchunk = lax.broadcasted_iota(jnp.int32, (tq, width), 0) // CHUNK
        return tuple(
            jnp.where((j * width + lax.broadcasted_iota(jnp.int32, (tq, width), 1)) // CHUNK <= qchunk, s, NEG)
            for j, s in enumerate(scores))

    def value_stage(h, off, width, scores, m, acc):
        m_new = m
        for s in scores:
            m_new = jnp.maximum(m_new, jnp.max(s, axis=-1, keepdims=True))
        pv = None
        for j, s in enumerate(scores):
            v = v_ref[0, h, pl.ds(pl.multiple_of(off + j * width, CHUNK), width), :]
            d = _dot(jnp.exp2(s - m_new).astype(BF16), v)
            pv = d if pv is None else pv + d
        return m_new, jnp.exp2(m - m_new) * acc + pv

    heads = range(hb)
    sub = min(tsup, SUB)
    ms = tuple(jnp.full((tq, 1), NEG, F32) for _ in heads)
    accs = tuple(jnp.zeros((tq, LANES), F32) for _ in heads)
    def sweep(i, carry):
        ms, accs = carry
        off = pl.multiple_of(i * tsup, tsup)
        out = [value_stage(h, off, sub, score_stage(h, off, tsup // sub, sub), ms[h], accs[h]) for h in heads]
        return tuple(o[0] for o in out), tuple(o[1] for o in out)

    ms, accs = lax.fori_loop(0, q_start // tsup if causal else lk // tsup, sweep, (ms, accs))
    if causal:
        dsub = min(dw, SUB)
        accs = [value_stage(h, q_start, dsub, chunk_mask(score_stage(h, q_start, dw // dsub, dsub), dsub),
                            ms[h], accs[h])[1] for h in heads]
    for h in heads:
        acc = accs[h]
        o_ref[0, h] = (acc / acc[:, HEAD_DIM:HEAD_DIM + 1])[:, :HEAD_DIM]


def _softmax_attn(q, k, v, *, hb, tq, tsup, dw=0, q0=0, causal):
    n, _, t, dk = q.shape
    lk = k.shape[2]
    nq = t // tq
    assert t % tq == 0 and tq % CHUNK == 0 and q0 % CHUNK == 0 and HEADS % hb == 0
    if causal:
        assert q0 % tsup == 0 and (nq == 1 or tq % tsup == 0) and q0 + (nq - 1) * tq + dw <= lk
    else:
        assert lk % tsup == 0
    return pl.pallas_call(
        functools.partial(_softmax_kernel, hb=hb, tq=tq, tsup=tsup, dw=dw, q0=q0, causal=causal, lk=lk),
        grid=(n, HEADS // hb, nq),
        in_specs=[pl.BlockSpec((1, hb, tq, dk), lambda i, h, j: (i, h, j, 0)),
                  _resident_spec((1, hb, lk, dk), nq), _resident_spec((1, hb, lk, LANES), nq)],
        out_specs=pl.BlockSpec((1, hb, tq, HEAD_DIM), lambda i, h, j: (i, h, j, 0)),
        out_shape=jax.ShapeDtypeStruct((n, HEADS, t, HEAD_DIM), F32),
        compiler_params=_params(3),
        name='softmax_attn',
    )(q, k, v)


def _augment_v(v):
    n, h, l, _ = v.shape
    return jnp.concatenate([v, jnp.ones((n, h, l, 1), v.dtype), jnp.zeros((n, h, l, LANES - HEAD_DIM - 1), v.dtype)],
                           axis=-1)


def _stick_kernel(q_ref, k_ref, v_ref, tri_ref, o_ref, *, hb, tq, tsup, dw, q0):
    q_start = pl.multiple_of(q0 + pl.program_id(2) * tq, CHUNK)

    def score_stage(h, off, n, width, masked):
        tri = tri_ref[...] if width == SUB else tri_ref[0:width, 0:width]
        q = q_ref[0, h]
        parts = []
        for j in range(n):
            k = k_ref[0, h, pl.ds(pl.multiple_of(off + j * width, CHUNK), width), :]
            nz = _dot_nt(q, k)
            neg_abs = lax.bitcast_convert_type(lax.bitcast_convert_type(nz, jnp.int32) | SIGN_BIT, F32)
            log_1m = jnp.minimum(nz, 0.0) - jnp.log(1.0 + jnp.exp2(neg_abs)) * LOG2E
            if masked:
                mask = (j * width + lax.broadcasted_iota(jnp.int32, (tq, width), 1)
                        < lax.broadcasted_iota(jnp.int32, (tq, width), 0))
                log_1m = jnp.where(mask, log_1m, 0.0)
            tail = _dot(log_1m.astype(BF16), tri)
            expo = log_1m - nz + tail
            if masked:
                expo = jnp.where(mask, expo, NEG)
            parts.append((expo, tail[:, 0:1] + log_1m[:, 0:1]))
        return tuple(parts)

    def value_stage(h, off, width, parts, run, acc):
        for j in reversed(range(len(parts))):
            expo, row_sum = parts[j]
            v = v_ref[0, h, pl.ds(pl.multiple_of(off + j * width, CHUNK), width), :]
            acc = acc + _dot(jnp.exp2(expo).astype(BF16), v) * jnp.exp2(run)
            run = run + row_sum
        return run, acc

    heads = range(hb)
    dsub = min(dw, SUB)
    runs = tuple(jnp.zeros((tq, 1), F32) for _ in heads)
    accs = tuple(jnp.zeros((tq, HEAD_DIM), F32) for _ in heads)
    diag = tuple(score_stage(h, q_start, dw // dsub, dsub, True) for h in heads)
    n_super = q_start // tsup

    out = [value_stage(h, q_start, dsub, diag[h], runs[h], accs[h]) for h in heads]
    runs, accs = tuple(o[0] for o in out), tuple(o[1] for o in out)

    def sweep(i, carry):
        runs, accs = carry
        off = pl.multiple_of((n_super - 1 - i) * tsup, tsup)
        out = [value_stage(h, off, SUB, score_stage(h, off, tsup // SUB, SUB, False), runs[h], accs[h])
               for h in heads]
        return tuple(o[0] for o in out), tuple(o[1] for o in out)

    _, accs = lax.fori_loop(0, n_super, sweep, (runs, accs))
    for h in heads:
        o_ref[0, h] = accs[h]


def _stick_attn(q, k, v, *, hb, tq, tsup, dw, q0=0):
    n, _, t, _ = q.shape
    lk = k.shape[2]
    nq = t // tq
    assert t % tq == 0 and q0 % CHUNK == 0 and tq % CHUNK == 0 and tsup % SUB == 0 and HEADS % hb == 0
    assert q0 % tsup == 0 and (nq == 1 or tq % tsup == 0) and q0 + (nq - 1) * tq + dw <= lk
    r = jnp.arange(SUB)
    tri = (r[:, None] > r[None, :]).astype(BF16)
    return pl.pallas_call(
        functools.partial(_stick_kernel, hb=hb, tq=tq, tsup=tsup, dw=dw, q0=q0),
        grid=(n, HEADS // hb, nq),
        in_specs=[pl.BlockSpec((1, hb, tq, HEAD_DIM), lambda i, h, j: (i, h, j, 0)),
                  _resident_spec((1, hb, lk, HEAD_DIM), nq), _resident_spec((1, hb, lk, HEAD_DIM), nq),
                  pl.BlockSpec((SUB, SUB), lambda i, h, j: (0, 0))],
        out_specs=pl.BlockSpec((1, hb, tq, HEAD_DIM), lambda i, h, j: (i, h, j, 0)),
        out_shape=jax.ShapeDtypeStruct((n, HEADS, t, HEAD_DIM), F32),
        compiler_params=_params(3),
        name='stick_attn',
    )(q, k, v, tri)


def _merge_kernel(x_ref, oa_ref, ob_ref, oc_ref, om_ref, gate_ref, og_ref, w_ref, y_ref):
    acc = x_ref[0]
    for gi, o_ref in enumerate((oa_ref, ob_ref, oc_ref, om_ref)):
        for h in range(HEADS):
            j = HEADS * gi + h
            o = o_ref[0, h]
            y = o * lax.rsqrt(jnp.mean(o * o, axis=-1, keepdims=True) + EPS) * og_ref[j] * gate_ref[0, j]
            acc = acc + _dot(y.astype(BF16), w_ref[j])
    y_ref[0] = acc


def _merge(x, o_a, o_b, o_c, o_m, gates, lw, tm):
    n, t, _ = x.shape
    hspec = pl.BlockSpec((1, HEADS, tm, HEAD_DIM), lambda i, j: (i, 0, j, 0))
    return pl.pallas_call(
        _merge_kernel,
        grid=(n, t // tm),
        in_specs=[pl.BlockSpec((1, tm, D_MODEL), lambda i, j: (i, j, 0)), hspec, hspec, hspec, hspec,
                  pl.BlockSpec((1, 4 * HEADS, tm, HEAD_DIM), lambda i, j: (i, 0, j, 0)),
                  pl.BlockSpec((4 * HEADS, 1, HEAD_DIM), lambda i, j: (0, 0, 0)),
                  pl.BlockSpec((4 * HEADS, HEAD_DIM, D_MODEL), lambda i, j: (0, 0, 0))],
        out_specs=pl.BlockSpec((1, tm, D_MODEL), lambda i, j: (i, j, 0)),
        out_shape=jax.ShapeDtypeStruct((n, t, D_MODEL), F32),
        compiler_params=_params(2),
        name='merge',
    )(x, o_a, o_b, o_c, o_m, gates, lw['out_g'], lw['w_out'])


def _block_diag(block, reps):
    return jnp.kron(jnp.eye(reps, dtype=F32), block)


def _layer_weights(l, p):
    w_in = p['w_in'][l]
    zeros = lambda c: jnp.zeros((D_MODEL, c), F32)
    kr0 = 1408
    w_in = jnp.concatenate([w_in[:, :kr0], zeros(ROPE_LANE0), w_in[:, kr0:kr0 + B_ROPE],
                            zeros(LANES - ROPE_LANE0 - B_ROPE), w_in[:, kr0 + B_ROPE:]], axis=1)
    assert w_in.shape[1] == IN_WIDTH_PADDED
    ones = lambda k: jnp.full((k, k), 1.0 / k, F32)
    zpad = lambda k: jnp.zeros((k, k), F32)
    tile = lambda g, reps=HEADS: jnp.tile(g, reps)[None, :]
    pad_to = lambda g, k: jnp.concatenate([g, jnp.zeros((k - g.shape[0],), F32)])
    wq = jnp.pad(p['b_wq_b'][l].reshape(Q_LORA, HEADS, B_QK), ((0, 0), (0, 0), (0, LANES - B_QK)))
    wkv = p['b_wkv_b'][l].reshape(KV_LORA, HEADS, B_NOPE + HEAD_DIM)
    wk = jnp.pad(wkv[:, :, :B_NOPE], ((0, 0), (0, 0), (0, LANES - B_NOPE))).reshape(KV_LORA, HEADS * LANES)
    wv = jnp.pad(wkv[:, :, B_NOPE:], ((0, 0), (0, 0), (0, LANES - HEAD_DIM))).reshape(KV_LORA, HEADS * LANES)
    gq128 = jax.scipy.linalg.block_diag(ones(B_NOPE), ones(B_ROPE), zpad(LANES - B_QK))
    gk128 = jax.scipy.linalg.block_diag(ones(B_NOPE), zpad(LANES - B_NOPE))
    place = jnp.zeros((B_ROPE, LANES), F32).at[jnp.arange(B_ROPE), ROPE_LANE0 + jnp.arange(B_ROPE)].set(1.0)
    return dict(
        norm_g=p['norm_g'][l][None, :],
        w_in=w_in.astype(BF16),
        g64=_block_diag(ones(HEAD_DIM), HEADS).astype(BF16),
        a_qn_g=tile(p['a_qn_g'][l]), a_kn_g=tile(p['a_kn_g'][l]),
        b_cq_g=p['b_cq_g'][l][None, :],
        wq=wq.reshape(Q_LORA, HEADS * LANES).astype(BF16),
        gq=_block_diag(gq128, 2).astype(BF16),
        gq_gain=tile(jnp.concatenate([p['b_qn_g'][l], pad_to(p['b_qr_g'][l], LANES - B_NOPE)])),
        b_ckv_g=p['b_ckv_g'][l][None, :],
        kr_gain=jnp.concatenate([jnp.zeros((ROPE_LANE0,), F32), pad_to(p['b_kr_g'][l], LANES - ROPE_LANE0)])[None, :],
        m_qn_g=tile(p['m_qn_g'][l]),
        wkv=jnp.concatenate([wk, wv], axis=1).astype(BF16),
        place=place.astype(BF16),
        gk=_block_diag(gk128, 2).astype(BF16),
        gk_gain=tile(pad_to(p['b_kn_g'][l], LANES)),
        v_one=jnp.zeros((1, LANES), F32).at[0, HEAD_DIM].set(1.0),
        m_norm_g=p['m_norm_g'][l][None, :],
        w_mem_kv=p['w_mem_kv'][l].astype(BF16),
        m_kn_g=tile(p['m_kn_g'][l]),
        out_g=p['out_g'][l].reshape(4 * HEADS, 1, HEAD_DIM),
        w_out=p['w_out'][l].reshape(4 * HEADS, HEAD_DIM, D_MODEL).astype(BF16),
        a_rel_bias=p['a_rel_bias'][l],
    )


def _rope_tables(pos):
    half = B_ROPE // 2
    freqs = ROPE_THETA ** (-jnp.arange(half, dtype=F32) / half)
    ang = pos.astype(F32)[:, None] * freqs[None, :]
    cos, sin = jnp.cos(ang), jnp.sin(ang)
    t = pos.shape[0]
    one, zero = jnp.ones((t, ROPE_LANE0), F32), jnp.zeros((t, ROPE_LANE0), F32)
    tail, z16 = jnp.zeros((t, LANES - ROPE_LANE0 - B_ROPE), F32), jnp.zeros((t, half), F32)
    c = jnp.concatenate([one, cos, cos, tail], axis=1)
    s1 = jnp.concatenate([zero, -sin, z16, tail], axis=1)
    s2 = jnp.concatenate([zero, z16, sin, tail], axis=1)
    return c, s1, s2


def _head_major(a):
    return a.transpose(0, 2, 1, 3)


def _token_major(a):
    return a.reshape(a.shape[0], a.shape[1], HEADS, HEAD_DIM)


def _pad_rows(a, front, back):
    return jnp.pad(a, ((0, 0), (0, 0), (front, back), (0, 0)))


def _tiles(s):
    tm = min(256, s)
    tq = min(PROMPT_TQ, s)
    g = min(4, s // CHUNK)
    return tm, tq, g


def _layer_prompt(x, mem, lw, rope_tabs):
    n, s, _ = x.shape
    tm, tq, g = _tiles(s)
    pr = _proj(x, rope_tabs, lw, tm)
    kb, vb = _kv_expand(pr['ckv'], pr['kr'], lw, min(1024, s))
    bias = _band_bias(lw['a_rel_bias'], g)
    o_a = _band_attn(pr['aq'], pr['ak'], pr['av'], bias, g, 0)
    o_b = _softmax_attn(pr['bq'], kb, vb, hb=PROMPT_HB, tq=tq, tsup=tq, dw=tq, causal=True)
    o_c = _stick_attn(pr['cq'], pr['ck'], pr['cv'], hb=PROMPT_HB, tq=tq, tsup=tq, dw=tq)
    mk, mv = _mem_kv(mem, lw)
    mk4, mv4 = _token_major(mk), _token_major(mv)
    o_m = _softmax_attn(pr['mq'], _head_major(mk4).astype(BF16), _augment_v(_head_major(mv4).astype(BF16)),
                        hb=HEADS, tq=tq, tsup=mk.shape[1], causal=False)
    y = _merge(x, o_a, o_b, o_c, o_m, pr['gates'], lw, tm)
    keep = min(A_WIN, s)
    state = (_token_major(pr['ak_s'][:, s - keep:]), _token_major(pr['av_s'][:, s - keep:]), pr['ckv'], pr['kr'],
             _token_major(pr['ck_s']), _token_major(pr['cv_s']), mk4, mv4)
    return y, state


def _layer_step(x, lw, rope_tabs, ca_k, ca_v, cb_ckv, cb_kr, cc_k, cc_v, cm_k, cm_v):
    n, t, _ = x.shape
    n_past = cb_ckv.shape[1]
    assert t == CHUNK and n_past % CHUNK == 0 and ca_k.shape[1] == A_WIN
    dw = 2 * CHUNK
    back = dw - t
    pr = _proj(x, rope_tabs, lw, t)
    ckv_all = jnp.pad(jnp.concatenate([cb_ckv, pr['ckv']], 1), ((0, 0), (0, back), (0, 0)))
    kr_all = jnp.pad(jnp.concatenate([cb_kr, pr['kr']], 1), ((0, 0), (0, back), (0, 0)))
    kb, vb = _kv_expand(ckv_all, kr_all, lw, ckv_all.shape[1])
    bias = _band_bias(lw['a_rel_bias'], 1)
    ak_all = jnp.concatenate([_head_major(ca_k).astype(BF16), pr['ak']], 2)
    av_all = jnp.concatenate([_head_major(ca_v).astype(BF16), pr['av']], 2)
    o_a = _band_attn(pr['aq'], ak_all, av_all, bias, 1, A_WIN)
    o_b = _softmax_attn(pr['bq'], kb, vb, hb=HEADS, tq=t, tsup=n_past, dw=dw, q0=n_past, causal=True)
    ck_all = _pad_rows(jnp.concatenate([_head_major(cc_k).astype(BF16), pr['ck']], 2), 0, back)
    cv_all = _pad_rows(jnp.concatenate([_head_major(cc_v).astype(BF16), pr['cv']], 2), 0, back)
    o_c = _stick_attn(pr['cq'], ck_all, cv_all, hb=HEADS, tq=t, tsup=n_past, dw=dw, q0=n_past)
    o_m = _softmax_attn(pr['mq'], _head_major(cm_k).astype(BF16), _augment_v(_head_major(cm_v).astype(BF16)),
                        hb=HEADS, tq=t, tsup=cm_k.shape[1], causal=False)
    y = _merge(x, o_a, o_b, o_c, o_m, pr['gates'], lw, t)
    state = (jnp.concatenate([ca_k[:, t:], _token_major(pr['ak_s'])], 1),
             jnp.concatenate([ca_v[:, t:], _token_major(pr['av_s'])], 1),
             pr['ckv'], pr['kr'], _token_major(pr['ck_s']), _token_major(pr['cv_s']))
    return y, state


def kernel(x_prompt, x_sample, mem_prompt, cache_a_k, cache_a_v, cache_b_ckv, cache_b_krope, cache_c_k, cache_c_v,
           cache_mem_k, cache_mem_v, norm_g, w_in, a_qn_g, a_kn_g, a_rel_bias, b_cq_g, b_wq_b, b_ckv_g, b_wkv_b,
           b_qn_g, b_qr_g, b_kn_g, b_kr_g, m_norm_g, w_mem_kv, m_qn_g, m_kn_g, out_g, w_out):
    p = dict(norm_g=norm_g, w_in=w_in, a_qn_g=a_qn_g, a_kn_g=a_kn_g, a_rel_bias=a_rel_bias, b_cq_g=b_cq_g,
             b_wq_b=b_wq_b, b_ckv_g=b_ckv_g, b_wkv_b=b_wkv_b, b_qn_g=b_qn_g, b_qr_g=b_qr_g, b_kn_g=b_kn_g,
             b_kr_g=b_kr_g, m_norm_g=m_norm_g, w_mem_kv=w_mem_kv, m_qn_g=m_qn_g, m_kn_g=m_kn_g, out_g=out_g,
             w_out=w_out)
    depth = w_in.shape[0]
    s, t, n_past = x_prompt.shape[1], x_sample.shape[1], cache_b_ckv.shape[2]
    tabs_p = _rope_tables(jnp.arange(s))
    tabs_s = _rope_tables(n_past + jnp.arange(t))
    hp, hs = x_prompt, x_sample
    p_states, s_states = [], []
    for l in range(depth):
        lw = _layer_weights(l, p)
        hp, sp = _layer_prompt(hp, mem_prompt, lw, tabs_p)
        hs, ss = _layer_step(hs, lw, tabs_s, cache_a_k[l], cache_a_v[l], cache_b_ckv[l], cache_b_krope[l],
                             cache_c_k[l], cache_c_v[l], cache_mem_k[l], cache_mem_v[l])
        p_states.append(sp)
        s_states.append(ss)
    outs_p = [jnp.stack([st[i] for st in p_states]) for i in range(8)]
    outs_s = [jnp.stack([st[i] for st in s_states]) for i in range(6)]
    return (hp, hs, *outs_p, *outs_s)
```

```python
import functools

import jax
import jax.numpy as jnp
from jax import lax
from jax.experimental import pallas as pl
from jax.experimental.pallas import tpu as pltpu

F32 = jnp.float32
BF16 = jnp.bfloat16

D_MODEL = 1024
HEAD_DIM = 64
HEADS = 4
GROUP = HEADS * HEAD_DIM
CHUNK = 64
A_PREV_CHUNKS = 8
A_WIN = A_PREV_CHUNKS * CHUNK
REL_CLIP = 256
B_NOPE = 64
B_ROPE = 32
B_QK = B_NOPE + B_ROPE
Q_LORA = 256
KV_LORA = 128
ROPE_THETA = 10000.0
MLA_SCALE = B_QK ** -0.5
HEAD_SCALE = HEAD_DIM ** -0.5
EPS = 1e-6
NEG = -1e30
LOG2E = 1.4426950408889634
SIGN_BIT = -2 ** 31
SUB = 256
PROMPT_TQ = 1024
PROMPT_HB = 2
LANES = 128
VMEM_LIMIT = 48 * 1024 * 1024

OFF_AQ, OFF_AK, OFF_AV, OFF_AG = 0, 256, 512, 768
OFF_BCQ, OFF_BCKV, OFF_BKR, OFF_BG = 1024, 1280, 1408, 1536
OFF_CQ, OFF_CK, OFF_CV, OFF_CG = 1792, 2048, 2304, 2560
OFF_MQ, OFF_MG = 2816, 3072
IN_WIDTH_PADDED = 3328
ROPE_LANE0 = B_NOPE


def _params(n_axes):
    return pltpu.CompilerParams(dimension_semantics=("arbitrary",) * n_axes, vmem_limit_bytes=VMEM_LIMIT)


def _dot(a, b):
    return jnp.dot(a, b, preferred_element_type=F32)


def _dot_nt(a, b):
    return lax.dot_general(a, b, (((1,), (1,)), ((), ())), preferred_element_type=F32)


def _group_mean(x2, g):
    return _dot(x2.astype(BF16), g)


def _rope(x, c, s1, s2):
    return x * c + pltpu.roll(x, LANES - B_ROPE // 2, 1) * s1 + pltpu.roll(x, B_ROPE // 2, 1) * s2


def _proj_kernel(x_ref, c_ref, s1_ref, s2_ref, ng_ref, win_ref, g64_ref, aqg_ref, akg_ref, bcqg_ref, wq_ref,
                 gq_ref, gqg_ref, ckvg_ref, krg_ref, mqg_ref,
                 aq_o, ak_o, av_o, aks_o, avs_o, bq_o, ckv_o, kr_o, cq_o, ck_o, cv_o, cks_o, cvs_o, mq_o, gate_o,
                 scr):
    x = x_ref[0]
    ms = jnp.mean(x * x, axis=-1, keepdims=True)
    xn = (x * lax.rsqrt(ms + EPS) * ng_ref[...]).astype(BF16)
    c, s1, s2 = c_ref[...], s1_ref[...], s2_ref[...]

    def seg(off, width):
        return _dot(xn, win_ref[:, off:off + width])

    def head_rms(h, gain_ref):
        m = _group_mean(h * h, g64_ref[...])
        return h * lax.rsqrt(m + EPS) * gain_ref[...]

    def put_heads(o_ref, val, base=0):
        scr[...] = val
        for h in range(HEADS):
            o_ref[0, base + h] = scr[:, h * HEAD_DIM:(h + 1) * HEAD_DIM].astype(o_ref.dtype)

    put_heads(aq_o, head_rms(seg(OFF_AQ, GROUP), aqg_ref) * HEAD_SCALE)
    ak = head_rms(seg(OFF_AK, GROUP), akg_ref)
    aks_o[0] = ak
    put_heads(ak_o, ak)
    av = seg(OFF_AV, GROUP)
    avs_o[0] = av
    put_heads(av_o, av)

    for gi, off in enumerate((OFF_AG, OFF_BG, OFF_CG, OFF_MG)):
        g = seg(off, GROUP)
        put_heads(gate_o, g * jax.nn.sigmoid(g), base=HEADS * gi)

    h = seg(OFF_BCQ, Q_LORA)
    cq = (h * lax.rsqrt(jnp.mean(h * h, axis=-1, keepdims=True) + EPS) * bcqg_ref[...]).astype(BF16)
    qb = _dot(cq, wq_ref[...])
    for half in range(2):
        part = qb[:, half * 2 * LANES:(half + 1) * 2 * LANES]
        m = _group_mean(part * part, gq_ref[...])
        normed = part * lax.rsqrt(m + EPS) * gqg_ref[:, half * 2 * LANES:(half + 1) * 2 * LANES]
        for hh in range(2):
            xh = normed[:, hh * LANES:(hh + 1) * LANES]
            bq_o[0, 2 * half + hh] = (_rope(xh, c, s1, s2) * (MLA_SCALE * LOG2E)).astype(BF16)

    h = seg(OFF_BCKV, KV_LORA)
    ckv_o[0] = h * lax.rsqrt(jnp.mean(h * h, axis=-1, keepdims=True) + EPS) * ckvg_ref[...]
    h = seg(OFF_BKR, LANES)
    ms = jnp.sum(h * h, axis=-1, keepdims=True) * (1.0 / B_ROPE)
    kr = _rope(h * lax.rsqrt(ms + EPS) * krg_ref[...], c, s1, s2)
    scr[:, 0:LANES] = kr
    kr_o[0] = scr[:, ROPE_LANE0:ROPE_LANE0 + B_ROPE]

    put_heads(cq_o, seg(OFF_CQ, GROUP) * (-HEAD_SCALE * LOG2E))
    ck = seg(OFF_CK, GROUP)
    cks_o[0] = ck
    put_heads(ck_o, ck)
    cv = seg(OFF_CV, GROUP)
    cvs_o[0] = cv
    put_heads(cv_o, cv)

    put_heads(mq_o, head_rms(seg(OFF_MQ, GROUP), mqg_ref) * (HEAD_SCALE * LOG2E))


def _proj(x, rope_tabs, lw, tm):
    n, t, _ = x.shape
    nt = t // tm
    c, s1, s2 = rope_tabs

    def full(a):
        return pl.BlockSpec(a.shape, lambda i, j: (0,) * a.ndim)

    tab_spec = pl.BlockSpec((tm, LANES), lambda i, j: (j, 0))
    weights = (lw['norm_g'], lw['w_in'], lw['g64'], lw['a_qn_g'], lw['a_kn_g'], lw['b_cq_g'], lw['wq'], lw['gq'],
               lw['gq_gain'], lw['b_ckv_g'], lw['kr_gain'], lw['m_qn_g'])

    def hm(width, dtype, heads=HEADS):
        return (jax.ShapeDtypeStruct((n, heads, t, width), dtype),
                pl.BlockSpec((1, heads, tm, width), lambda i, j: (i, 0, j, 0)))

    def tmaj(width):
        return (jax.ShapeDtypeStruct((n, t, width), F32), pl.BlockSpec((1, tm, width), lambda i, j: (i, j, 0)))

    outs = [hm(HEAD_DIM, BF16), hm(HEAD_DIM, BF16), hm(HEAD_DIM, BF16), tmaj(GROUP), tmaj(GROUP),
            hm(LANES, BF16), tmaj(KV_LORA), tmaj(B_ROPE),
            hm(HEAD_DIM, BF16), hm(HEAD_DIM, BF16), hm(HEAD_DIM, BF16), tmaj(GROUP), tmaj(GROUP),
            hm(HEAD_DIM, BF16), hm(HEAD_DIM, F32, 4 * HEADS)]
    names = ('aq', 'ak', 'av', 'ak_s', 'av_s', 'bq', 'ckv', 'kr', 'cq', 'ck', 'cv', 'ck_s', 'cv_s', 'mq', 'gates')
    res = pl.pallas_call(
        _proj_kernel,
        grid=(n, nt),
        in_specs=[pl.BlockSpec((1, tm, D_MODEL), lambda i, j: (i, j, 0)), tab_spec, tab_spec, tab_spec]
        + [full(w) for w in weights],
        out_specs=[o[1] for o in outs],
        out_shape=[o[0] for o in outs],
        scratch_shapes=[pltpu.VMEM((tm, GROUP), F32)],
        compiler_params=_params(2),
        name='proj',
    )(x, c, s1, s2, *weights)
    return dict(zip(names, res))


def _kv_kernel(ckv_ref, kr_ref, wkv_ref, place_ref, gk_ref, gkg_ref, vone_ref, k_o, v_o):
    kv = _dot(ckv_ref[0].astype(BF16), wkv_ref[...])
    kr = _dot(kr_ref[0].astype(BF16), place_ref[...])
    for half in range(2):
        part = kv[:, half * 2 * LANES:(half + 1) * 2 * LANES]
        m = _group_mean(part * part, gk_ref[...])
        normed = part * lax.rsqrt(m + EPS) * gkg_ref[:, half * 2 * LANES:(half + 1) * 2 * LANES]
        for hh in range(2):
            k_o[0, 2 * half + hh] = (normed[:, hh * LANES:(hh + 1) * LANES] + kr).astype(BF16)
    for h in range(HEADS):
        v_o[0, h] = (kv[:, (HEADS + h) * LANES:(HEADS + h + 1) * LANES] + vone_ref[...]).astype(BF16)


def _kv_expand(ckv, kr, lw, tl):
    n, l, _ = ckv.shape
    weights = (lw['wkv'], lw['place'], lw['gk'], lw['gk_gain'], lw['v_one'])
    return pl.pallas_call(
        _kv_kernel,
        grid=(n, l // tl),
        in_specs=[pl.BlockSpec((1, tl, KV_LORA), lambda i, j: (i, j, 0)),
                  pl.BlockSpec((1, tl, B_ROPE), lambda i, j: (i, j, 0))]
        + [pl.BlockSpec(w.shape, lambda i, j: (0, 0)) for w in weights],
        out_specs=[pl.BlockSpec((1, HEADS, tl, LANES), lambda i, j: (i, 0, j, 0)),
                   pl.BlockSpec((1, HEADS, tl, LANES), lambda i, j: (i, 0, j, 0))],
        out_shape=[jax.ShapeDtypeStruct((n, HEADS, l, LANES), BF16)] * 2,
        compiler_params=_params(2),
        name='kv_expand',
    )(ckv, kr, *weights)


def _memkv_kernel(mem_ref, ng_ref, w_ref, g64_ref, kg_ref, k_o, v_o):
    x = mem_ref[0]
    xn = (x * lax.rsqrt(jnp.mean(x * x, axis=-1, keepdims=True) + EPS) * ng_ref[...]).astype(BF16)
    kv = _dot(xn, w_ref[...])
    k = kv[:, :GROUP]
    m = _group_mean(k * k, g64_ref[...])
    k_o[0] = k * lax.rsqrt(m + EPS) * kg_ref[...]
    v_o[0] = kv[:, GROUP:]


def _mem_kv(mem, lw):
    n, nm, _ = mem.shape
    weights = (lw['m_norm_g'], lw['w_mem_kv'], lw['g64'], lw['m_kn_g'])
    return pl.pallas_call(
        _memkv_kernel,
        grid=(n,),
        in_specs=[pl.BlockSpec((1, nm, D_MODEL), lambda i: (i, 0, 0))]
        + [pl.BlockSpec(w.shape, lambda i: (0, 0)) for w in weights],
        out_specs=[pl.BlockSpec((1, nm, GROUP), lambda i: (i, 0, 0))] * 2,
        out_shape=[jax.ShapeDtypeStruct((n, nm, GROUP), F32)] * 2,
        compiler_params=_params(1),
        name='mem_kv',
    )(mem, *weights)


def _band_kernel(q_ref, k_ref, v_ref, bias_ref, o_ref, *, rows, span, front):
    step = pl.program_id(1)
    start = pl.multiple_of(step * rows, CHUNK)

    def attend(first_key, width):
        for h in range(HEADS):
            k = k_ref[0, h, pl.ds(first_key, width), :]
            v = v_ref[0, h, pl.ds(first_key, width), :]
            s = _dot_nt(q_ref[0, h], k) + bias_ref[h, :, span - width:]
            p = jnp.exp(s - jnp.max(s, axis=-1, keepdims=True))
            l = jnp.sum(p, axis=-1, keepdims=True)
            o_ref[0, h] = _dot(p.astype(BF16), v) / l

    early = (A_WIN - front) // rows
    for e in range(early):
        pl.when(step == e)(functools.partial(attend, 0, front + (e + 1) * rows))
    pl.when(step >= early)(lambda: attend(pl.multiple_of(start + front - A_WIN, CHUNK), span))


def _band_attn(q, k, v, bias, g, front):
    n, _, t, _ = q.shape
    rows, span = CHUNK * g, CHUNK * (g + A_PREV_CHUNKS)
    lk = k.shape[2]
    assert (A_WIN - front) % rows == 0 and lk == front + t
    resident = dict(pipeline_mode=pl.Buffered(1)) if t // rows > 1 else {}
    return pl.pallas_call(
        functools.partial(_band_kernel, rows=rows, span=span, front=front),
        grid=(n, t // rows),
        in_specs=[pl.BlockSpec((1, HEADS, rows, HEAD_DIM), lambda i, j: (i, 0, j, 0)),
                  pl.BlockSpec((1, HEADS, lk, HEAD_DIM), lambda i, j: (i, 0, 0, 0), **resident),
                  pl.BlockSpec((1, HEADS, lk, HEAD_DIM), lambda i, j: (i, 0, 0, 0), **resident),
                  pl.BlockSpec((HEADS, rows, span), lambda i, j: (0, 0, 0))],
        out_specs=pl.BlockSpec((1, HEADS, rows, HEAD_DIM), lambda i, j: (i, 0, j, 0)),
        out_shape=jax.ShapeDtypeStruct((n, HEADS, t, HEAD_DIM), F32),
        compiler_params=_params(2),
        name='band_attn',
    )(q, k, v, bias)


def _band_bias(tab, g):
    rows, span = CHUNK * g, CHUNK * (g + A_PREV_CHUNKS)
    i = jnp.arange(rows)[:, None]
    j = jnp.arange(span)[None, :]
    lo = (i // CHUNK) * CHUNK
    inband = (j >= lo) & (j < lo + CHUNK * (A_PREV_CHUNKS + 1))
    period = rows + span - 1
    u = jnp.concatenate([jnp.arange(span), jnp.arange(-(rows - 1), 0)])
    line = tab[:, jnp.clip(A_WIN - u, -REL_CLIP, REL_CLIP) + REL_CLIP].astype(F32)
    skew = jnp.tile(line, (1, rows))[:, :rows * (period - 1)].reshape(-1, rows, period - 1)
    return jnp.where(inband[None], skew[:, :, :span], NEG)


def _resident_spec(block, nq):
    mode = dict(pipeline_mode=pl.Buffered(1)) if nq > 1 else {}
    return pl.BlockSpec(block, lambda i, h, j: (i, h, 0, 0), **mode)


def _softmax_kernel(q_ref, k_ref, v_ref, o_ref, *, hb, tq, tsup, dw, q0, causal, lk):
    q_start = pl.multiple_of(q0 + pl.program_id(2) * tq, CHUNK)

    def score_stage(h, off, n, width):
        q = q_ref[0, h]
        return tuple(_dot_nt(q, k_ref[0, h, pl.ds(pl.multiple_of(off + j * width, CHUNK), width), :])
                     for j in range(n))

    def chunk_mask(scores, width):
        qchunk = lax.broadcasted_iota(jnp.int32, (tq, width), 0) // CHUNK
        return tuple(
            jnp.where((j * width + lax.broadcasted_iota(jnp.int32, (tq, width), 1)) // CHUNK <= qchunk, s, NEG)
            for j, s in enumerate(scores))

    def value_stage(h, off, width, scores, m, acc):
        m_new = m
        for s in scores:
            m_new = jnp.maximum(m_new, jnp.max(s, axis=-1, keepdims=True))
        pv = None
        for j, s in enumerate(scores):
            v = v_ref[0, h, pl.ds(pl.multiple_of(off + j * width, CHUNK), width), :]
            d = _dot(jnp.exp2(s - m_new).astype(BF16), v)
            pv = d if pv is None else pv + d
        return m_new, jnp.exp2(m - m_new) * acc + pv

    heads = range(hb)
    sub = min(tsup, SUB)
    ms = tuple(jnp.full((tq, 1), NEG, F32) for _ in heads)
    accs = tuple(jnp.zeros((tq, LANES), F32) for _ in heads)
    def sweep(i, carry):
        ms, accs = carry
        off = pl.multiple_of(i * tsup, tsup)
        out = [value_stage(h, off, sub, score_stage(h, off, tsup // sub, sub), ms[h], accs[h]) for h in heads]
        return tuple(o[0] for o in out), tuple(o[1] for o in out)

    ms, accs = lax.fori_loop(0, q_start // tsup if causal else lk // tsup, sweep, (ms, accs))
    if causal:
        dsub = min(dw, SUB)
        accs = [value_stage(h, q_start, dsub, chunk_mask(score_stage(h, q_start, dw // dsub, dsub), dsub),
                            ms[h], accs[h])[1] for h in heads]
    ones_lane = lax.broadcasted_iota(jnp.int32, (tq, LANES), 1) == HEAD_DIM
    for h in heads:
        acc = accs[h]
        denom = jnp.sum(jnp.where(ones_lane, acc, 0.0), axis=-1, keepdims=True)
        o_ref[0, h] = (acc * (1.0 / denom))[:, :HEAD_DIM]


def _softmax_attn(q, k, v, *, hb, tq, tsup, dw=0, q0=0, causal):
    n, _, t, dk = q.shape
    lk = k.shape[2]
    nq = t // tq
    assert t % tq == 0 and tq % CHUNK == 0 and q0 % CHUNK == 0 and HEADS % hb == 0
    if causal:
        assert q0 % tsup == 0 and (nq == 1 or tq % tsup == 0) and q0 + (nq - 1) * tq + dw <= lk
    else:
        assert lk % tsup == 0
    return pl.pallas_call(
        functools.partial(_softmax_kernel, hb=hb, tq=tq, tsup=tsup, dw=dw, q0=q0, causal=causal, lk=lk),
        grid=(n, HEADS // hb, nq),
        in_specs=[pl.BlockSpec((1, hb, tq, dk), lambda i, h, j: (i, h, j, 0)),
                  _resident_spec((1, hb, lk, dk), nq), _resident_spec((1, hb, lk, LANES), nq)],
        out_specs=pl.BlockSpec((1, hb, tq, HEAD_DIM), lambda i, h, j: (i, h, j, 0)),
        out_shape=jax.ShapeDtypeStruct((n, HEADS, t, HEAD_DIM), F32),
        compiler_params=_params(3),
        name='softmax_attn',
    )(q, k, v)


def _augment_v(v):
    n, h, l, _ = v.shape
    return jnp.concatenate([v, jnp.ones((n, h, l, 1), v.dtype), jnp.zeros((n, h, l, LANES - HEAD_DIM - 1), v.dtype)],
                           axis=-1)


def _stick_kernel(q_ref, k_ref, v_ref, tri_ref, o_ref, *, hb, tq, tsup, dw, q0):
    q_start = pl.multiple_of(q0 + pl.program_id(2) * tq, CHUNK)

    def score_stage(h, off, n, width, masked):
        tri = tri_ref[...] if width == SUB else tri_ref[0:width, 0:width]
        q = q_ref[0, h]
        parts = []
        for j in range(n):
            k = k_ref[0, h, pl.ds(pl.multiple_of(off + j * width, CHUNK), width), :]
            nz = _dot_nt(q, k)
            neg_abs = lax.bitcast_convert_type(lax.bitcast_convert_type(nz, jnp.int32) | SIGN_BIT, F32)
            log_1m = jnp.minimum(nz, 0.0) - jnp.log(1.0 + jnp.exp2(neg_abs)) * LOG2E
            if masked:
                mask = (j * width + lax.broadcasted_iota(jnp.int32, (tq, width), 1)
                        < lax.broadcasted_iota(jnp.int32, (tq, width), 0))
                log_1m = jnp.where(mask, log_1m, 0.0)
            tail = _dot(log_1m.astype(BF16), tri)
            expo = log_1m - nz + tail
            if masked:
                expo = jnp.where(mask, expo, NEG)
            parts.append((expo, tail[:, 0:1] + log_1m[:, 0:1]))
        return tuple(parts)

    def value_stage(h, off, width, parts, run, acc):
        for j in reversed(range(len(parts))):
            expo, row_sum = parts[j]
            v = v_ref[0, h, pl.ds(pl.multiple_of(off + j * width, CHUNK), width), :]
            acc = acc + _dot(jnp.exp2(expo).astype(BF16), v) * jnp.exp2(run)
            run = run + row_sum
        return run, acc

    heads = range(hb)
    dsub = min(dw, SUB)
    runs = tuple(jnp.zeros((tq, 1), F32) for _ in heads)
    accs = tuple(jnp.zeros((tq, HEAD_DIM), F32) for _ in heads)
    diag = tuple(score_stage(h, q_start, dw // dsub, dsub, True) for h in heads)
    n_super = q_start // tsup

    out = [value_stage(h, q_start, dsub, diag[h], runs[h], accs[h]) for h in heads]
    runs, accs = tuple(o[0] for o in out), tuple(o[1] for o in out)

    def sweep(i, carry):
        runs, accs = carry
        off = pl.multiple_of((n_super - 1 - i) * tsup, tsup)
        out = [value_stage(h, off, SUB, score_stage(h, off, tsup // SUB, SUB, False), runs[h], accs[h])
               for h in heads]
        return tuple(o[0] for o in out), tuple(o[1] for o in out)

    _, accs = lax.fori_loop(0, n_super, sweep, (runs, accs))
    for h in heads:
        o_ref[0, h] = accs[h]


def _stick_attn(q, k, v, *, hb, tq, tsup, dw, q0=0):
    n, _, t, _ = q.shape
    lk = k.shape[2]
    nq = t // tq
    assert t % tq == 0 and q0 % CHUNK == 0 and tq % CHUNK == 0 and tsup % SUB == 0 and HEADS % hb == 0
    assert q0 % tsup == 0 and (nq == 1 or tq % tsup == 0) and q0 + (nq - 1) * tq + dw <= lk
    r = jnp.arange(SUB)
    tri = (r[:, None] > r[None, :]).astype(BF16)
    return pl.pallas_call(
        functools.partial(_stick_kernel, hb=hb, tq=tq, tsup=tsup, dw=dw, q0=q0),
        grid=(n, HEADS // hb, nq),
        in_specs=[pl.BlockSpec((1, hb, tq, HEAD_DIM), lambda i, h, j: (i, h, j, 0)),
                  _resident_spec((1, hb, lk, HEAD_DIM), nq), _resident_spec((1, hb, lk, HEAD_DIM), nq),
                  pl.BlockSpec((SUB, SUB), lambda i, h, j: (0, 0))],
        out_specs=pl.BlockSpec((1, hb, tq, HEAD_DIM), lambda i, h, j: (i, h, j, 0)),
        out_shape=jax.ShapeDtypeStruct((n, HEADS, t, HEAD_DIM), F32),
        compiler_params=_params(3),
        name='stick_attn',
    )(q, k, v, tri)


def _merge_kernel(x_ref, oa_ref, ob_ref, oc_ref, om_ref, gate_ref, og_ref, w_ref, y_ref):
    acc = x_ref[0]
    for gi, o_ref in enumerate((oa_ref, ob_ref, oc_ref, om_ref)):
        for h in range(HEADS):
            j = HEADS * gi + h
            o = o_ref[0, h]
            y = o * lax.rsqrt(jnp.mean(o * o, axis=-1, keepdims=True) + EPS) * og_ref[j] * gate_ref[0, j]
            acc = acc + _dot(y.astype(BF16), w_ref[j])
    y_ref[0] = acc


def _merge(x, o_a, o_b, o_c, o_m, gates, lw, tm):
    n, t, _ = x.shape
    hspec = pl.BlockSpec((1, HEADS, tm, HEAD_DIM), lambda i, j: (i, 0, j, 0))
    return pl.pallas_call(
        _merge_kernel,
        grid=(n, t // tm),
        in_specs=[pl.BlockSpec((1, tm, D_MODEL), lambda i, j: (i, j, 0)), hspec, hspec, hspec, hspec,
                  pl.BlockSpec((1, 4 * HEADS, tm, HEAD_DIM), lambda i, j: (i, 0, j, 0)),
                  pl.BlockSpec((4 * HEADS, 1, HEAD_DIM), lambda i, j: (0, 0, 0)),
                  pl.BlockSpec((4 * HEADS, HEAD_DIM, D_MODEL), lambda i, j: (0, 0, 0))],
        out_specs=pl.BlockSpec((1, tm, D_MODEL), lambda i, j: (i, j, 0)),
        out_shape=jax.ShapeDtypeStruct((n, t, D_MODEL), F32),
        compiler_params=_params(2),
        name='merge',
    )(x, o_a, o_b, o_c, o_m, gates, lw['out_g'], lw['w_out'])


def _block_diag(block, reps):
    return jnp.kron(jnp.eye(reps, dtype=F32), block)


def _layer_weights(l, p):
    w_in = p['w_in'][l]
    zeros = lambda c: jnp.zeros((D_MODEL, c), F32)
    kr0 = 1408
    w_in = jnp.concatenate([w_in[:, :kr0], zeros(ROPE_LANE0), w_in[:, kr0:kr0 + B_ROPE],
                            zeros(LANES - ROPE_LANE0 - B_ROPE), w_in[:, kr0 + B_ROPE:]], axis=1)
    assert w_in.shape[1] == IN_WIDTH_PADDED
    ones = lambda k: jnp.full((k, k), 1.0 / k, F32)
    zpad = lambda k: jnp.zeros((k, k), F32)
    tile = lambda g, reps=HEADS: jnp.tile(g, reps)[None, :]
    pad_to = lambda g, k: jnp.concatenate([g, jnp.zeros((k - g.shape[0],), F32)])
    wq = jnp.pad(p['b_wq_b'][l].reshape(Q_LORA, HEADS, B_QK), ((0, 0), (0, 0), (0, LANES - B_QK)))
    wkv = p['b_wkv_b'][l].reshape(KV_LORA, HEADS, B_NOPE + HEAD_DIM)
    wk = jnp.pad(wkv[:, :, :B_NOPE], ((0, 0), (0, 0), (0, LANES - B_NOPE))).reshape(KV_LORA, HEADS * LANES)
    wv = jnp.pad(wkv[:, :, B_NOPE:], ((0, 0), (0, 0), (0, LANES - HEAD_DIM))).reshape(KV_LORA, HEADS * LANES)
    gq128 = jax.scipy.linalg.block_diag(ones(B_NOPE), ones(B_ROPE), zpad(LANES - B_QK))
    gk128 = jax.scipy.linalg.block_diag(ones(B_NOPE), zpad(LANES - B_NOPE))
    place = jnp.zeros((B_ROPE, LANES), F32).at[jnp.arange(B_ROPE), ROPE_LANE0 + jnp.arange(B_ROPE)].set(1.0)
    return dict(
        norm_g=p['norm_g'][l][None, :],
        w_in=w_in.astype(BF16),
        g64=_block_diag(ones(HEAD_DIM), HEADS).astype(BF16),
        a_qn_g=tile(p['a_qn_g'][l]), a_kn_g=tile(p['a_kn_g'][l]),
        b_cq_g=p['b_cq_g'][l][None, :],
        wq=wq.reshape(Q_LORA, HEADS * LANES).astype(BF16),
        gq=_block_diag(gq128, 2).astype(BF16),
        gq_gain=tile(jnp.concatenate([p['b_qn_g'][l], pad_to(p['b_qr_g'][l], LANES - B_NOPE)])),
        b_ckv_g=p['b_ckv_g'][l][None, :],
        kr_gain=jnp.concatenate([jnp.zeros((ROPE_LANE0,), F32), pad_to(p['b_kr_g'][l], LANES - ROPE_LANE0)])[None, :],
        m_qn_g=tile(p['m_qn_g'][l]),
        wkv=jnp.concatenate([wk, wv], axis=1).astype(BF16),
        place=place.astype(BF16),
        gk=_block_diag(gk128, 2).astype(BF16),
        gk_gain=tile(pad_to(p['b_kn_g'][l], LANES)),
        v_one=jnp.zeros((1, LANES), F32).at[0, HEAD_DIM].set(1.0),
        m_norm_g=p['m_norm_g'][l][None, :],
        w_mem_kv=p['w_mem_kv'][l].astype(BF16),
        m_kn_g=tile(p['m_kn_g'][l]),
        out_g=p['out_g'][l].reshape(4 * HEADS, 1, HEAD_DIM),
        w_out=p['w_out'][l].reshape(4 * HEADS, HEAD_DIM, D_MODEL).astype(BF16),
        a_rel_bias=p['a_rel_bias'][l],
    )


def _rope_tables(pos):
    half = B_ROPE // 2
    freqs = ROPE_THETA ** (-jnp.arange(half, dtype=F32) / half)
    ang = pos.astype(F32)[:, None] * freqs[None, :]
    cos, sin = jnp.cos(ang), jnp.sin(ang)
    t = pos.shape[0]
    one, zero = jnp.ones((t, ROPE_LANE0), F32), jnp.zeros((t, ROPE_LANE0), F32)
    tail, z16 = jnp.zeros((t, LANES - ROPE_LANE0 - B_ROPE), F32), jnp.zeros((t, half), F32)
    c = jnp.concatenate([one, cos, cos, tail], axis=1)
    s1 = jnp.concatenate([zero, -sin, z16, tail], axis=1)
    s2 = jnp.concatenate([zero, z16, sin, tail], axis=1)
    return c, s1, s2


def _head_major(a):
    return a.transpose(0, 2, 1, 3)


def _token_major(a):
    return a.reshape(a.shape[0], a.shape[1], HEADS, HEAD_DIM)


def _pad_rows(a, front, back):
    return jnp.pad(a, ((0, 0), (0, 0), (front, back), (0, 0)))


def _tiles(s):
    tm = min(512, s)
    tq = min(PROMPT_TQ, s)
    g = min(4, s // CHUNK)
    return tm, tq, g


def _layer_prompt(x, mem, lw, rope_tabs):
    n, s, _ = x.shape
    tm, tq, g = _tiles(s)
    pr = _proj(x, rope_tabs, lw, tm)
    kb, vb = _kv_expand(pr['ckv'], pr['kr'], lw, min(1024, s))
    bias = _band_bias(lw['a_rel_bias'], g)
    o_a = _band_attn(pr['aq'], pr['ak'], pr['av'], bias, g, 0)
    o_b = _softmax_attn(pr['bq'], kb, vb, hb=PROMPT_HB, tq=tq, tsup=tq, dw=tq, causal=True)
    o_c = _stick_attn(pr['cq'], pr['ck'], pr['cv'], hb=PROMPT_HB, tq=tq, tsup=tq, dw=tq)
    mk, mv = _mem_kv(mem, lw)
    mk4, mv4 = _token_major(mk), _token_major(mv)
    o_m = _softmax_attn(pr['mq'], _head_major(mk4).astype(BF16), _augment_v(_head_major(mv4).astype(BF16)),
                        hb=HEADS, tq=tq, tsup=mk.shape[1], causal=False)
    y = _merge(x, o_a, o_b, o_c, o_m, pr['gates'], lw, tm)
    keep = min(A_WIN, s)
    state = (_token_major(pr['ak_s'][:, s - keep:]), _token_major(pr['av_s'][:, s - keep:]), pr['ckv'], pr['kr'],
             _token_major(pr['ck_s']), _token_major(pr['cv_s']), mk4, mv4)
    return y, state


def _layer_step(x, lw, rope_tabs, ca_k, ca_v, cb_ckv, cb_kr, cc_k, cc_v, cm_k, cm_v):
    n, t, _ = x.shape
    n_past = cb_ckv.shape[1]
    assert t == CHUNK and n_past % CHUNK == 0 and ca_k.shape[1] == A_WIN
    dw = 2 * CHUNK
    back = dw - t
    pr = _proj(x, rope_tabs, lw, t)
    ckv_all = jnp.pad(jnp.concatenate([cb_ckv, pr['ckv']], 1), ((0, 0), (0, back), (0, 0)))
    kr_all = jnp.pad(jnp.concatenate([cb_kr, pr['kr']], 1), ((0, 0), (0, back), (0, 0)))
    kb, vb = _kv_expand(ckv_all, kr_all, lw, ckv_all.shape[1])
    bias = _band_bias(lw['a_rel_bias'], 1)
    ak_all = jnp.concatenate([_head_major(ca_k).astype(BF16), pr['ak']], 2)
    av_all = jnp.concatenate([_head_major(ca_v).astype(BF16), pr['av']], 2)
    o_a = _band_attn(pr['aq'], ak_all, av_all, bias, 1, A_WIN)
    o_b = _softmax_attn(pr['bq'], kb, vb, hb=HEADS, tq=t, tsup=n_past, dw=dw, q0=n_past, causal=True)
    ck_all = _pad_rows(jnp.concatenate([_head_major(cc_k).astype(BF16), pr['ck']], 2), 0, back)
    cv_all = _pad_rows(jnp.concatenate([_head_major(cc_v).astype(BF16), pr['cv']], 2), 0, back)
    o_c = _stick_attn(pr['cq'], ck_all, cv_all, hb=HEADS, tq=t, tsup=n_past, dw=dw, q0=n_past)
    o_m = _softmax_attn(pr['mq'], _head_major(cm_k).astype(BF16), _augment_v(_head_major(cm_v).astype(BF16)),
                        hb=HEADS, tq=t, tsup=cm_k.shape[1], causal=False)
    y = _merge(x, o_a, o_b, o_c, o_m, pr['gates'], lw, t)
    state = (jnp.concatenate([ca_k[:, t:], _token_major(pr['ak_s'])], 1),
             jnp.concatenate([ca_v[:, t:], _token_major(pr['av_s'])], 1),
             pr['ckv'], pr['kr'], _token_major(pr['ck_s']), _token_major(pr['cv_s']))
    return y, state


def kernel(x_prompt, x_sample, mem_prompt, cache_a_k, cache_a_v, cache_b_ckv, cache_b_krope, cache_c_k, cache_c_v,
           cache_mem_k, cache_mem_v, norm_g, w_in, a_qn_g, a_kn_g, a_rel_bias, b_cq_g, b_wq_b, b_ckv_g, b_wkv_b,
           b_qn_g, b_qr_g, b_kn_g, b_kr_g, m_norm_g, w_mem_kv, m_qn_g, m_kn_g, out_g, w_out):
    p = dict(norm_g=norm_g, w_in=w_in, a_qn_g=a_qn_g, a_kn_g=a_kn_g, a_rel_bias=a_rel_bias, b_cq_g=b_cq_g,
             b_wq_b=b_wq_b, b_ckv_g=b_ckv_g, b_wkv_b=b_wkv_b, b_qn_g=b_qn_g, b_qr_g=b_qr_g, b_kn_g=b_kn_g,
             b_kr_g=b_kr_g, m_norm_g=m_norm_g, w_mem_kv=w_mem_kv, m_qn_g=m_qn_g, m_kn_g=m_kn_g, out_g=out_g,
             w_out=w_out)
    depth = w_in.shape[0]
    s, t, n_past = x_prompt.shape[1], x_sample.shape[1], cache_b_ckv.shape[2]
    tabs_p = _rope_tables(jnp.arange(s))
    tabs_s = _rope_tables(n_past + jnp.arange(t))
    hp, hs = x_prompt, x_sample
    p_states, s_states = [], []
    for l in range(depth):
        lw = _layer_weights(l, p)
        hp, sp = _layer_prompt(hp, mem_prompt, lw, tabs_p)
        hs, ss = _layer_step(hs, lw, tabs_s, cache_a_k[l], cache_a_v[l], cache_b_ckv[l], cache_b_krope[l],
                             cache_c_k[l], cache_c_v[l], cache_mem_k[l], cache_mem_v[l])
        p_states.append(sp)
        s_states.append(ss)
    outs_p = [jnp.stack([st[i] for st in p_states]) for i in range(8)]
    outs_s = [jnp.stack([st[i] for st in s_states]) for i in range(6)]
    return (hp, hs, *outs_p, *outs_s)
```

```python
import functools

import jax
import jax.numpy as jnp
from jax import lax
from jax.experimental import pallas as pl
from jax.experimental.pallas import tpu as pltpu

F32 = jnp.float32
BF16 = jnp.bfloat16

D_MODEL = 1024
HEAD_DIM = 64
HEADS = 4
GROUP = HEADS * HEAD_DIM
CHUNK = 64
A_PREV_CHUNKS = 8
A_WIN = A_PREV_CHUNKS * CHUNK
REL_CLIP = 256
B_NOPE = 64
B_ROPE = 32
B_QK = B_NOPE + B_ROPE
Q_LORA = 256
KV_LORA = 128
ROPE_THETA = 10000.0
MLA_SCALE = B_QK ** -0.5
HEAD_SCALE = HEAD_DIM ** -0.5
EPS = 1e-6
NEG = -1e30
LOG2E = 1.4426950408889634
SIGN_BIT = -2 ** 31
SUB = 256
PROMPT_TQ = 1024
PROMPT_HB = 2
LANES = 128
VMEM_LIMIT = 48 * 1024 * 1024

OFF_AQ, OFF_AK, OFF_AV, OFF_AG = 0, 256, 512, 768
OFF_BCQ, OFF_BCKV, OFF_BKR, OFF_BG = 1024, 1280, 1408, 1536
OFF_CQ, OFF_CK, OFF_CV, OFF_CG = 1792, 2048, 2304, 2560
OFF_MQ, OFF_MG = 2816, 3072
IN_WIDTH_PADDED = 3328
ROPE_LANE0 = B_NOPE


def _params(n_axes):
    return pltpu.CompilerParams(dimension_semantics=("arbitrary",) * n_axes, vmem_limit_bytes=VMEM_LIMIT)


def _dot(a, b):
    return jnp.dot(a, b, preferred_element_type=F32)


def _dot_nt(a, b):
    return lax.dot_general(a, b, (((1,), (1,)), ((), ())), preferred_element_type=F32)


def _group_mean(x2, g):
    return _dot(x2.astype(BF16), g)


def _rope(x, c, s1, s2):
    return x * c + pltpu.roll(x, LANES - B_ROPE // 2, 1) * s1 + pltpu.roll(x, B_ROPE // 2, 1) * s2


def _proj_kernel(x_ref, c_ref, s1_ref, s2_ref, ng_ref, win_ref, g64_ref, aqg_ref, akg_ref, bcqg_ref, wq_ref,
                 gq_ref, gqg_ref, ckvg_ref, krg_ref, mqg_ref,
                 aq_o, ak_o, av_o, aks_o, avs_o, bq_o, ckv_o, kr_o, cq_o, ck_o, cv_o, cks_o, cvs_o, mq_o, gate_o,
                 scr):
    x = x_ref[0]
    ms = jnp.mean(x * x, axis=-1, keepdims=True)
    xn = (x * lax.rsqrt(ms + EPS) * ng_ref[...]).astype(BF16)
    c, s1, s2 = c_ref[...], s1_ref[...], s2_ref[...]

    def seg(off, width):
        return _dot(xn, win_ref[:, off:off + width])

    def head_rms(h, gain_ref):
        m = _group_mean(h * h, g64_ref[...])
        return h * lax.rsqrt(m + EPS) * gain_ref[...]

    def put_heads(o_ref, val, base=0):
        scr[...] = val
        for h in range(HEADS):
            o_ref[0, base + h] = scr[:, h * HEAD_DIM:(h + 1) * HEAD_DIM].astype(o_ref.dtype)

    put_heads(aq_o, head_rms(seg(OFF_AQ, GROUP), aqg_ref) * HEAD_SCALE)
    ak = head_rms(seg(OFF_AK, GROUP), akg_ref)
    aks_o[0] = ak
    put_heads(ak_o, ak)
    av = seg(OFF_AV, GROUP)
    avs_o[0] = av
    put_heads(av_o, av)

    for gi, off in enumerate((OFF_AG, OFF_BG, OFF_CG, OFF_MG)):
        g = seg(off, GROUP)
        put_heads(gate_o, g * jax.nn.sigmoid(g), base=HEADS * gi)

    h = seg(OFF_BCQ, Q_LORA)
    cq = (h * lax.rsqrt(jnp.mean(h * h, axis=-1, keepdims=True) + EPS) * bcqg_ref[...]).astype(BF16)
    qb = _dot(cq, wq_ref[...])
    for half in range(2):
        part = qb[:, half * 2 * LANES:(half + 1) * 2 * LANES]
        m = _group_mean(part * part, gq_ref[...])
        normed = part * lax.rsqrt(m + EPS) * gqg_ref[:, half * 2 * LANES:(half + 1) * 2 * LANES]
        for hh in range(2):
            xh = normed[:, hh * LANES:(hh + 1) * LANES]
            bq_o[0, 2 * half + hh] = (_rope(xh, c, s1, s2) * (MLA_SCALE * LOG2E)).astype(BF16)

    h = seg(OFF_BCKV, KV_LORA)
    ckv_o[0] = h * lax.rsqrt(jnp.mean(h * h, axis=-1, keepdims=True) + EPS) * ckvg_ref[...]
    h = seg(OFF_BKR, LANES)
    ms = jnp.sum(h * h, axis=-1, keepdims=True) * (1.0 / B_ROPE)
    kr = _rope(h * lax.rsqrt(ms + EPS) * krg_ref[...], c, s1, s2)
    scr[:, 0:LANES] = kr
    kr_o[0] = scr[:, ROPE_LANE0:ROPE_LANE0 + B_ROPE]

    put_heads(cq_o, seg(OFF_CQ, GROUP) * (-HEAD_SCALE * LOG2E))
    ck = seg(OFF_CK, GROUP)
    cks_o[0] = ck
    put_heads(ck_o, ck)
    cv = seg(OFF_CV, GROUP)
    cvs_o[0] = cv
    put_heads(cv_o, cv)

    put_heads(mq_o, head_rms(seg(OFF_MQ, GROUP), mqg_ref) * (HEAD_SCALE * LOG2E))


def _proj(x, rope_tabs, lw, tm):
    n, t, _ = x.shape
    nt = t // tm
    c, s1, s2 = rope_tabs

    def full(a):
        return pl.BlockSpec(a.shape, lambda i, j: (0,) * a.ndim)

    tab_spec = pl.BlockSpec((tm, LANES), lambda i, j: (j, 0))
    weights = (lw['norm_g'], lw['w_in'], lw['g64'], lw['a_qn_g'], lw['a_kn_g'], lw['b_cq_g'], lw['wq'], lw['gq'],
               lw['gq_gain'], lw['b_ckv_g'], lw['kr_gain'], lw['m_qn_g'])

    def hm(width, dtype, heads=HEADS):
        return (jax.ShapeDtypeStruct((n, heads, t, width), dtype),
                pl.BlockSpec((1, heads, tm, width), lambda i, j: (i, 0, j, 0)))

    def tmaj(width):
        return (jax.ShapeDtypeStruct((n, t, width), F32), pl.BlockSpec((1, tm, width), lambda i, j: (i, j, 0)))

    outs = [hm(HEAD_DIM, BF16), hm(HEAD_DIM, BF16), hm(HEAD_DIM, BF16), tmaj(GROUP), tmaj(GROUP),
            hm(LANES, BF16), tmaj(KV_LORA), tmaj(B_ROPE),
            hm(HEAD_DIM, BF16), hm(HEAD_DIM, BF16), hm(HEAD_DIM, BF16), tmaj(GROUP), tmaj(GROUP),
            hm(HEAD_DIM, BF16), hm(HEAD_DIM, F32, 4 * HEADS)]
    names = ('aq', 'ak', 'av', 'ak_s', 'av_s', 'bq', 'ckv', 'kr', 'cq', 'ck', 'cv', 'ck_s', 'cv_s', 'mq', 'gates')
    res = pl.pallas_call(
        _proj_kernel,
        grid=(n, nt),
        in_specs=[pl.BlockSpec((1, tm, D_MODEL), lambda i, j: (i, j, 0)), tab_spec, tab_spec, tab_spec]
        + [full(w) for w in weights],
        out_specs=[o[1] for o in outs],
        out_shape=[o[0] for o in outs],
        scratch_shapes=[pltpu.VMEM((tm, GROUP), F32)],
        compiler_params=_params(2),
        name='proj',
    )(x, c, s1, s2, *weights)
    return dict(zip(names, res))


def _kv_kernel(ckv_ref, kr_ref, wkv_ref, place_ref, gk_ref, gkg_ref, vone_ref, k_o, v_o):
    kv = _dot(ckv_ref[0].astype(BF16), wkv_ref[...])
    kr = _dot(kr_ref[0].astype(BF16), place_ref[...])
    for half in range(2):
        part = kv[:, half * 2 * LANES:(half + 1) * 2 * LANES]
        m = _group_mean(part * part, gk_ref[...])
        normed = part * lax.rsqrt(m + EPS) * gkg_ref[:, half * 2 * LANES:(half + 1) * 2 * LANES]
        for hh in range(2):
            k_o[0, 2 * half + hh] = (normed[:, hh * LANES:(hh + 1) * LANES] + kr).astype(BF16)
    for h in range(HEADS):
        v_o[0, h] = (kv[:, (HEADS + h) * LANES:(HEADS + h + 1) * LANES] + vone_ref[...]).astype(BF16)


def _kv_expand(ckv, kr, lw, tl):
    n, l, _ = ckv.shape
    weights = (lw['wkv'], lw['place'], lw['gk'], lw['gk_gain'], lw['v_one'])
    return pl.pallas_call(
        _kv_kernel,
        grid=(n, l // tl),
        in_specs=[pl.BlockSpec((1, tl, KV_LORA), lambda i, j: (i, j, 0)),
                  pl.BlockSpec((1, tl, B_ROPE), lambda i, j: (i, j, 0))]
        + [pl.BlockSpec(w.shape, lambda i, j: (0, 0)) for w in weights],
        out_specs=[pl.BlockSpec((1, HEADS, tl, LANES), lambda i, j: (i, 0, j, 0)),
                   pl.BlockSpec((1, HEADS, tl, LANES), lambda i, j: (i, 0, j, 0))],
        out_shape=[jax.ShapeDtypeStruct((n, HEADS, l, LANES), BF16)] * 2,
        compiler_params=_params(2),
        name='kv_expand',
    )(ckv, kr, *weights)


def _memkv_kernel(mem_ref, ng_ref, w_ref, g64_ref, kg_ref, k_o, v_o):
    x = mem_ref[0]
    xn = (x * lax.rsqrt(jnp.mean(x * x, axis=-1, keepdims=True) + EPS) * ng_ref[...]).astype(BF16)
    kv = _dot(xn, w_ref[...])
    k = kv[:, :GROUP]
    m = _group_mean(k * k, g64_ref[...])
    k_o[0] = k * lax.rsqrt(m + EPS) * kg_ref[...]
    v_o[0] = kv[:, GROUP:]


def _mem_kv(mem, lw):
    n, nm, _ = mem.shape
    weights = (lw['m_norm_g'], lw['w_mem_kv'], lw['g64'], lw['m_kn_g'])
    return pl.pallas_call(
        _memkv_kernel,
        grid=(n,),
        in_specs=[pl.BlockSpec((1, nm, D_MODEL), lambda i: (i, 0, 0))]
        + [pl.BlockSpec(w.shape, lambda i: (0, 0)) for w in weights],
        out_specs=[pl.BlockSpec((1, nm, GROUP), lambda i: (i, 0, 0))] * 2,
        out_shape=[jax.ShapeDtypeStruct((n, nm, GROUP), F32)] * 2,
        compiler_params=_params(1),
        name='mem_kv',
    )(mem, *weights)


def _band_kernel(q_ref, k_ref, v_ref, bias_ref, o_ref, *, rows, span, front):
    step = pl.program_id(1)
    start = pl.multiple_of(step * rows, CHUNK)

    def attend(first_key, width):
        heads = range(HEADS)
        keys = pl.ds(first_key, width)
        s = [_dot_nt(q_ref[0, h], k_ref[0, h, keys, :]) + bias_ref[h, :, span - width:] for h in heads]
        p = [jnp.exp(s[h] - jnp.max(s[h], axis=-1, keepdims=True)) for h in heads]
        pv = [_dot(p[h].astype(BF16), v_ref[0, h, keys, :]) for h in heads]
        for h in heads:
            o_ref[0, h] = pv[h] / jnp.sum(p[h], axis=-1, keepdims=True)

    early = (A_WIN - front) // rows
    for e in range(early):
        pl.when(step == e)(functools.partial(attend, 0, front + (e + 1) * rows))
    pl.when(step >= early)(lambda: attend(pl.multiple_of(start + front - A_WIN, CHUNK), span))


def _band_attn(q, k, v, bias, g, front):
    n, _, t, _ = q.shape
    rows, span = CHUNK * g, CHUNK * (g + A_PREV_CHUNKS)
    lk = k.shape[2]
    assert (A_WIN - front) % rows == 0 and lk == front + t
    resident = dict(pipeline_mode=pl.Buffered(1)) if t // rows > 1 else {}
    return pl.pallas_call(
        functools.partial(_band_kernel, rows=rows, span=span, front=front),
        grid=(n, t // rows),
        in_specs=[pl.BlockSpec((1, HEADS, rows, HEAD_DIM), lambda i, j: (i, 0, j, 0)),
                  pl.BlockSpec((1, HEADS, lk, HEAD_DIM), lambda i, j: (i, 0, 0, 0), **resident),
                  pl.BlockSpec((1, HEADS, lk, HEAD_DIM), lambda i, j: (i, 0, 0, 0), **resident),
                  pl.BlockSpec((HEADS, rows, span), lambda i, j: (0, 0, 0))],
        out_specs=pl.BlockSpec((1, HEADS, rows, HEAD_DIM), lambda i, j: (i, 0, j, 0)),
        out_shape=jax.ShapeDtypeStruct((n, HEADS, t, HEAD_DIM), F32),
        compiler_params=_params(2),
        name='band_attn',
    )(q, k, v, bias)


def _band_bias(tab, g):
    rows, span = CHUNK * g, CHUNK * (g + A_PREV_CHUNKS)
    i = jnp.arange(rows)[:, None]
    j = jnp.arange(span)[None, :]
    lo = (i // CHUNK) * CHUNK
    inband = (j >= lo) & (j < lo + CHUNK * (A_PREV_CHUNKS + 1))
    period = rows + span - 1
    u = jnp.concatenate([jnp.arange(span), jnp.arange(-(rows - 1), 0)])
    line = tab[:, jnp.clip(A_WIN - u, -REL_CLIP, REL_CLIP) + REL_CLIP].astype(F32)
    skew = jnp.tile(line, (1, rows))[:, :rows * (period - 1)].reshape(-1, rows, period - 1)
    return jnp.where(inband[None], skew[:, :, :span], NEG)


def _resident_spec(block, nq):
    mode = dict(pipeline_mode=pl.Buffered(1)) if nq > 1 else {}
    return pl.BlockSpec(block, lambda i, h, j: (i, h, 0, 0), **mode)


def _softmax_kernel(q_ref, k_ref, v_ref, o_ref, *, hb, tq, tsup, dw, q0, causal, lk):
    q_start = pl.multiple_of(q0 + pl.program_id(2) * tq, CHUNK)

    def score_stage(h, off, n, width):
        q = q_ref[0, h]
        return tuple(_dot_nt(q, k_ref[0, h, pl.ds(pl.multiple_of(off + j * width, CHUNK), width), :])
                     for j in range(n))

    def chunk_mask(scores, width):
        qchunk = lax.broadcasted_iota(jnp.int32, (tq, width), 0) // CHUNK
        return tuple(
            jnp.where((j * width + lax.broadcasted_iota(jnp.int32, (tq, width), 1)) // CHUNK <= qchunk, s, NEG)
            for j, s in enumerate(scores))

    def value_stage(h, off, width, scores, m, acc):
        m_new = m
        for s in scores:
            m_new = jnp.maximum(m_new, jnp.max(s, axis=-1, keepdims=True))
        pv = None
        for j, s in enumerate(scores):
            v = v_ref[0, h, pl.ds(pl.multiple_of(off + j * width, CHUNK), width), :]
            d = _dot(jnp.exp2(s - m_new).astype(BF16), v)
            pv = d if pv is None else pv + d
        return m_new, jnp.exp2(m - m_new) * acc + pv

    heads = range(hb)
    sub = min(tsup, SUB)
    ms = tuple(jnp.full((tq, 1), NEG, F32) for _ in heads)
    accs = tuple(jnp.zeros((tq, LANES), F32) for _ in heads)
    def sweep(i, carry):
        ms, accs = carry
        off = pl.multiple_of(i * tsup, tsup)
        scores = [score_stage(h, off, tsup // sub, sub) for h in heads]
        out = [value_stage(h, off, sub, scores[h], ms[h], accs[h]) for h in heads]
        return tuple(o[0] for o in out), tuple(o[1] for o in out)

    ms, accs = lax.fori_loop(0, q_start // tsup if causal else lk // tsup, sweep, (ms, accs))
    if causal:
        dsub = min(dw, SUB)
        scores = [chunk_mask(score_stage(h, q_start, dw // dsub, dsub), dsub) for h in heads]
        accs = [value_stage(h, q_start, dsub, scores[h], ms[h], accs[h])[1] for h in heads]
    ones_lane = lax.broadcasted_iota(jnp.int32, (tq, LANES), 1) == HEAD_DIM
    for h in heads:
        acc = accs[h]
        denom = jnp.sum(jnp.where(ones_lane, acc, 0.0), axis=-1, keepdims=True)
        o_ref[0, h] = (acc * (1.0 / denom))[:, :HEAD_DIM]


def _softmax_attn(q, k, v, *, hb, tq, tsup, dw=0, q0=0, causal):
    n, _, t, dk = q.shape
    lk = k.shape[2]
    nq = t // tq
    assert t % tq == 0 and tq % CHUNK == 0 and q0 % CHUNK == 0 and HEADS % hb == 0
    if causal:
        assert q0 % tsup == 0 and (nq == 1 or tq % tsup == 0) and q0 + (nq - 1) * tq + dw <= lk
    else:
        assert lk % tsup == 0
    return pl.pallas_call(
        functools.partial(_softmax_kernel, hb=hb, tq=tq, tsup=tsup, dw=dw, q0=q0, causal=causal, lk=lk),
        grid=(n, HEADS // hb, nq),
        in_specs=[pl.BlockSpec((1, hb, tq, dk), lambda i, h, j: (i, h, j, 0)),
                  _resident_spec((1, hb, lk, dk), nq), _resident_spec((1, hb, lk, LANES), nq)],
        out_specs=pl.BlockSpec((1, hb, tq, HEAD_DIM), lambda i, h, j: (i, h, j, 0)),
        out_shape=jax.ShapeDtypeStruct((n, HEADS, t, HEAD_DIM), F32),
        compiler_params=_params(3),
        name='softmax_attn',
    )(q, k, v)


def _augment_v(v):
    n, h, l, _ = v.shape
    return jnp.concatenate([v, jnp.ones((n, h, l, 1), v.dtype), jnp.zeros((n, h, l, LANES - HEAD_DIM - 1), v.dtype)],
                           axis=-1)


def _stick_kernel(q_ref, k_ref, v_ref, tri_ref, o_ref, *, hb, tq, tsup, dw, q0):
    q_start = pl.multiple_of(q0 + pl.program_id(2) * tq, CHUNK)

    def score_stage(h, off, n, width, masked):
        tri = tri_ref[...] if width == SUB else tri_ref[0:width, 0:width]
        q = q_ref[0, h]
        parts = []
        for j in range(n):
            k = k_ref[0, h, pl.ds(pl.multiple_of(off + j * width, CHUNK), width), :]
            nz = _dot_nt(q, k)
            neg_abs = lax.bitcast_convert_type(lax.bitcast_convert_type(nz, jnp.int32) | SIGN_BIT, F32)
            log_1m = jnp.minimum(nz, 0.0) - jnp.log(1.0 + jnp.exp2(neg_abs)) * LOG2E
            if masked:
                mask = (j * width + lax.broadcasted_iota(jnp.int32, (tq, width), 1)
                        < lax.broadcasted_iota(jnp.int32, (tq, width), 0))
                log_1m = jnp.where(mask, log_1m, 0.0)
            tail = _dot(log_1m.astype(BF16), tri)
            expo = log_1m - nz + tail
            if masked:
                expo = jnp.where(mask, expo, NEG)
            parts.append((expo, jnp.sum(log_1m, axis=-1, keepdims=True)))
        return tuple(parts)

    def value_stage(h, off, width, parts, run, acc):
        for j in reversed(range(len(parts))):
            expo, row_sum = parts[j]
            v = v_ref[0, h, pl.ds(pl.multiple_of(off + j * width, CHUNK), width), :]
            acc = acc + _dot(jnp.exp2(expo).astype(BF16), v) * jnp.exp2(run)
            run = run + row_sum
        return run, acc

    heads = range(hb)
    dsub = min(dw, SUB)
    runs = tuple(jnp.zeros((tq, 1), F32) for _ in heads)
    accs = tuple(jnp.zeros((tq, HEAD_DIM), F32) for _ in heads)
    diag = tuple(score_stage(h, q_start, dw // dsub, dsub, True) for h in heads)
    n_super = q_start // tsup

    out = [value_stage(h, q_start, dsub, diag[h], runs[h], accs[h]) for h in heads]
    runs, accs = tuple(o[0] for o in out), tuple(o[1] for o in out)

    def sweep(i, carry):
        runs, accs = carry
        off = pl.multiple_of((n_super - 1 - i) * tsup, tsup)
        out = [value_stage(h, off, SUB, score_stage(h, off, tsup // SUB, SUB, False), runs[h], accs[h])
               for h in heads]
        return tuple(o[0] for o in out), tuple(o[1] for o in out)

    _, accs = lax.fori_loop(0, n_super, sweep, (runs, accs))
    for h in heads:
        o_ref[0, h] = accs[h]


def _stick_attn(q, k, v, *, hb, tq, tsup, dw, q0=0):
    n, _, t, _ = q.shape
    lk = k.shape[2]
    nq = t // tq
    assert t % tq == 0 and q0 % CHUNK == 0 and tq % CHUNK == 0 and tsup % SUB == 0 and HEADS % hb == 0
    assert q0 % tsup == 0 and (nq == 1 or tq % tsup == 0) and q0 + (nq - 1) * tq + dw <= lk
    r = jnp.arange(SUB)
    tri = (r[:, None] > r[None, :]).astype(BF16)
    return pl.pallas_call(
        functools.partial(_stick_kernel, hb=hb, tq=tq, tsup=tsup, dw=dw, q0=q0),
        grid=(n, HEADS // hb, nq),
        in_specs=[pl.BlockSpec((1, hb, tq, HEAD_DIM), lambda i, h, j: (i, h, j, 0)),
                  _resident_spec((1, hb, lk, HEAD_DIM), nq), _resident_spec((1, hb, lk, HEAD_DIM), nq),
                  pl.BlockSpec((SUB, SUB), lambda i, h, j: (0, 0))],
        out_specs=pl.BlockSpec((1, hb, tq, HEAD_DIM), lambda i, h, j: (i, h, j, 0)),
        out_shape=jax.ShapeDtypeStruct((n, HEADS, t, HEAD_DIM), F32),
        compiler_params=_params(3),
        name='stick_attn',
    )(q, k, v, tri)


def _step_attn_kernel(aq_ref, cq_ref, mq_ref, ak_t_ref, av_t_ref, ak_n_ref, av_n_ref, ck_t_ref, cv_t_ref,
                      ck_n_ref, cv_n_ref, mk_t_ref, mv_t_ref, bias_ref, tri_ref, oa_ref, oc_ref, om_ref, *, t):
    n_a, n_c = ak_t_ref.shape[-1], ck_t_ref.shape[-1]
    before = (lax.broadcasted_iota(jnp.int32, (t, t), 1) < lax.broadcasted_iota(jnp.int32, (t, t), 0))
    heads = range(HEADS)
    subs = range(n_c // SUB)

    def log_one_minus(nz):
        neg_abs = lax.bitcast_convert_type(lax.bitcast_convert_type(nz, jnp.int32) | SIGN_BIT, F32)
        return jnp.minimum(nz, 0.0) - jnp.log(1.0 + jnp.exp2(neg_abs)) * LOG2E

    a_c = [_dot(aq_ref[0, h], ak_t_ref[0, h].astype(BF16)) + bias_ref[h, :, :n_a] for h in heads]
    a_n = [_dot_nt(aq_ref[0, h], ak_n_ref[0, h]) + bias_ref[h, :, n_a:] for h in heads]
    m_s = [_dot(mq_ref[0, h], mk_t_ref[0, h].astype(BF16)) for h in heads]
    c_n = [_dot_nt(cq_ref[0, h], ck_n_ref[0, h]) for h in heads]
    c_c = [[_dot(cq_ref[0, h], ck_t_ref[0, h, :, j * SUB:(j + 1) * SUB].astype(BF16)) for j in subs] for h in heads]
    a_m = [jnp.maximum(jnp.max(a_c[h], axis=-1, keepdims=True), jnp.max(a_n[h], axis=-1, keepdims=True))
           for h in heads]
    a_pc = [jnp.exp(a_c[h] - a_m[h]) for h in heads]
    a_pn = [jnp.exp(a_n[h] - a_m[h]) for h in heads]
    m_p = [jnp.exp2(m_s[h] - jnp.max(m_s[h], axis=-1, keepdims=True)) for h in heads]
    l_n = [jnp.where(before, log_one_minus(c_n[h]), 0.0) for h in heads]
    l_c = [[log_one_minus(c_c[h][j]) for j in subs] for h in heads]
    for h in heads:
        pv = (_dot_nt(a_pc[h].astype(BF16), av_t_ref[0, h].astype(BF16))
              + _dot(a_pn[h].astype(BF16), av_n_ref[0, h]))
        oa_ref[0, h] = pv / (jnp.sum(a_pc[h], axis=-1, keepdims=True) + jnp.sum(a_pn[h], axis=-1, keepdims=True))
        om_ref[0, h] = (_dot_nt(m_p[h].astype(BF16), mv_t_ref[0, h].astype(BF16))
                        / jnp.sum(m_p[h], axis=-1, keepdims=True))
    t_n = [_dot(l_n[h].astype(BF16), tri_ref[0:t, 0:t]) for h in heads]
    t_c = [[_dot(l_c[h][j].astype(BF16), tri_ref[...]) for j in subs] for h in heads]
    w_n = [jnp.exp2(jnp.where(before, l_n[h] - c_n[h] + t_n[h], NEG)).astype(BF16) for h in heads]
    w_c = [[jnp.exp2(l_c[h][j] - c_c[h][j] + t_c[h][j]).astype(BF16) for j in subs] for h in heads]
    pv_n = [_dot(w_n[h], cv_n_ref[0, h]) for h in heads]
    pv_c = [[_dot_nt(w_c[h][j], cv_t_ref[0, h, :, j * SUB:(j + 1) * SUB].astype(BF16)) for j in subs]
            for h in heads]
    for h in heads:
        acc = pv_n[h]
        run = jnp.sum(l_n[h], axis=-1, keepdims=True)
        for j in reversed(subs):
            acc = acc + pv_c[h][j] * jnp.exp2(run)
            run = run + jnp.sum(l_c[h][j], axis=-1, keepdims=True)
        oc_ref[0, h] = acc


def _step_attn(pr, ak_t, av_t, ck_t, cv_t, mk_t, mv_t, bias):
    n, _, t, _ = pr['aq'].shape
    assert ck_t.shape[-1] % SUB == 0
    r = jnp.arange(SUB)
    tri = (r[:, None] > r[None, :]).astype(BF16)
    ops = (pr['aq'], pr['cq'], pr['mq'], ak_t, av_t, pr['ak'], pr['av'], ck_t, cv_t, pr['ck'], pr['cv'], mk_t, mv_t)

    def row_spec(a):
        return pl.BlockSpec((1,) + a.shape[1:], lambda i: (i, 0, 0, 0))

    out = jax.ShapeDtypeStruct((n, HEADS, t, HEAD_DIM), F32)
    return pl.pallas_call(
        functools.partial(_step_attn_kernel, t=t),
        grid=(n,),
        in_specs=[row_spec(a) for a in ops]
        + [pl.BlockSpec(bias.shape, lambda i: (0, 0, 0)), pl.BlockSpec((SUB, SUB), lambda i: (0, 0))],
        out_specs=[row_spec(out)] * 3,
        out_shape=[out] * 3,
        compiler_params=_params(1),
        name='step_attn',
    )(*ops, bias, tri)


def _merge_kernel(x_ref, oa_ref, ob_ref, oc_ref, om_ref, gate_ref, og_ref, w_ref, y_ref):
    acc = x_ref[0]
    for gi, o_ref in enumerate((oa_ref, ob_ref, oc_ref, om_ref)):
        for h in range(HEADS):
            j = HEADS * gi + h
            o = o_ref[0, h]
            y = o * lax.rsqrt(jnp.mean(o * o, axis=-1, keepdims=True) + EPS) * og_ref[j] * gate_ref[0, j]
            acc = acc + _dot(y.astype(BF16), w_ref[j])
    y_ref[0] = acc


def _merge(x, o_a, o_b, o_c, o_m, gates, lw, tm):
    n, t, _ = x.shape
    hspec = pl.BlockSpec((1, HEADS, tm, HEAD_DIM), lambda i, j: (i, 0, j, 0))
    return pl.pallas_call(
        _merge_kernel,
        grid=(n, t // tm),
        in_specs=[pl.BlockSpec((1, tm, D_MODEL), lambda i, j: (i, j, 0)), hspec, hspec, hspec, hspec,
                  pl.BlockSpec((1, 4 * HEADS, tm, HEAD_DIM), lambda i, j: (i, 0, j, 0)),
                  pl.BlockSpec((4 * HEADS, 1, HEAD_DIM), lambda i, j: (0, 0, 0)),
                  pl.BlockSpec((4 * HEADS, HEAD_DIM, D_MODEL), lambda i, j: (0, 0, 0))],
        out_specs=pl.BlockSpec((1, tm, D_MODEL), lambda i, j: (i, j, 0)),
        out_shape=jax.ShapeDtypeStruct((n, t, D_MODEL), F32),
        compiler_params=_params(2),
        name='merge',
    )(x, o_a, o_b, o_c, o_m, gates, lw['out_g'], lw['w_out'])


def _block_diag(block, reps):
    return jnp.kron(jnp.eye(reps, dtype=F32), block)


def _layer_weights(l, p):
    w_in = p['w_in'][l]
    zeros = lambda c: jnp.zeros((D_MODEL, c), F32)
    kr0 = 1408
    w_in = jnp.concatenate([w_in[:, :kr0], zeros(ROPE_LANE0), w_in[:, kr0:kr0 + B_ROPE],
                            zeros(LANES - ROPE_LANE0 - B_ROPE), w_in[:, kr0 + B_ROPE:]], axis=1)
    assert w_in.shape[1] == IN_WIDTH_PADDED
    ones = lambda k: jnp.full((k, k), 1.0 / k, F32)
    zpad = lambda k: jnp.zeros((k, k), F32)
    tile = lambda g, reps=HEADS: jnp.tile(g, reps)[None, :]
    pad_to = lambda g, k: jnp.concatenate([g, jnp.zeros((k - g.shape[0],), F32)])
    wq = jnp.pad(p['b_wq_b'][l].reshape(Q_LORA, HEADS, B_QK), ((0, 0), (0, 0), (0, LANES - B_QK)))
    wkv = p['b_wkv_b'][l].reshape(KV_LORA, HEADS, B_NOPE + HEAD_DIM)
    wk = jnp.pad(wkv[:, :, :B_NOPE], ((0, 0), (0, 0), (0, LANES - B_NOPE))).reshape(KV_LORA, HEADS * LANES)
    wv = jnp.pad(wkv[:, :, B_NOPE:], ((0, 0), (0, 0), (0, LANES - HEAD_DIM))).reshape(KV_LORA, HEADS * LANES)
    gq128 = jax.scipy.linalg.block_diag(ones(B_NOPE), ones(B_ROPE), zpad(LANES - B_QK))
    gk128 = jax.scipy.linalg.block_diag(ones(B_NOPE), zpad(LANES - B_NOPE))
    place = jnp.zeros((B_ROPE, LANES), F32).at[jnp.arange(B_ROPE), ROPE_LANE0 + jnp.arange(B_ROPE)].set(1.0)
    return dict(
        norm_g=p['norm_g'][l][None, :],
        w_in=w_in.astype(BF16),
        g64=_block_diag(ones(HEAD_DIM), HEADS).astype(BF16),
        a_qn_g=tile(p['a_qn_g'][l]), a_kn_g=tile(p['a_kn_g'][l]),
        b_cq_g=p['b_cq_g'][l][None, :],
        wq=wq.reshape(Q_LORA, HEADS * LANES).astype(BF16),
        gq=_block_diag(gq128, 2).astype(BF16),
        gq_gain=tile(jnp.concatenate([p['b_qn_g'][l], pad_to(p['b_qr_g'][l], LANES - B_NOPE)])),
        b_ckv_g=p['b_ckv_g'][l][None, :],
        kr_gain=jnp.concatenate([jnp.zeros((ROPE_LANE0,), F32), pad_to(p['b_kr_g'][l], LANES - ROPE_LANE0)])[None, :],
        m_qn_g=tile(p['m_qn_g'][l]),
        wkv=jnp.concatenate([wk, wv], axis=1).astype(BF16),
        place=place.astype(BF16),
        gk=_block_diag(gk128, 2).astype(BF16),
        gk_gain=tile(pad_to(p['b_kn_g'][l], LANES)),
        v_one=jnp.zeros((1, LANES), F32).at[0, HEAD_DIM].set(1.0),
        m_norm_g=p['m_norm_g'][l][None, :],
        w_mem_kv=p['w_mem_kv'][l].astype(BF16),
        m_kn_g=tile(p['m_kn_g'][l]),
        out_g=p['out_g'][l].reshape(4 * HEADS, 1, HEAD_DIM),
        w_out=p['w_out'][l].reshape(4 * HEADS, HEAD_DIM, D_MODEL).astype(BF16),
        a_rel_bias=p['a_rel_bias'][l],
    )


def _rope_tables(pos):
    half = B_ROPE // 2
    freqs = ROPE_THETA ** (-jnp.arange(half, dtype=F32) / half)
    ang = pos.astype(F32)[:, None] * freqs[None, :]
    cos, sin = jnp.cos(ang), jnp.sin(ang)
    t = pos.shape[0]
    one, zero = jnp.ones((t, ROPE_LANE0), F32), jnp.zeros((t, ROPE_LANE0), F32)
    tail, z16 = jnp.zeros((t, LANES - ROPE_LANE0 - B_ROPE), F32), jnp.zeros((t, half), F32)
    c = jnp.concatenate([one, cos, cos, tail], axis=1)
    s1 = jnp.concatenate([zero, -sin, z16, tail], axis=1)
    s2 = jnp.concatenate([zero, z16, sin, tail], axis=1)
    return c, s1, s2


def _head_major(a):
    return a.transpose(0, 2, 1, 3)


def _token_major(a):
    return a.reshape(a.shape[0], a.shape[1], HEADS, HEAD_DIM)


def _pad_rows(a, front, back):
    return jnp.pad(a, ((0, 0), (0, 0), (front, back), (0, 0)))


def _tiles(s):
    tm = min(512, s)
    tq = min(PROMPT_TQ, s)
    g = min(4, s // CHUNK)
    return tm, tq, g


def _layer_prompt(x, mem, lw, rope_tabs):
    n, s, _ = x.shape
    tm, tq, g = _tiles(s)
    pr = _proj(x, rope_tabs, lw, tm)
    kb, vb = _kv_expand(pr['ckv'], pr['kr'], lw, min(1024, s))
    bias = _band_bias(lw['a_rel_bias'], g)
    o_a = _band_attn(pr['aq'], pr['ak'], pr['av'], bias, g, 0)
    o_b = _softmax_attn(pr['bq'], kb, vb, hb=PROMPT_HB, tq=tq, tsup=tq, dw=tq, causal=True)
    o_c = _stick_attn(pr['cq'], pr['ck'], pr['cv'], hb=PROMPT_HB, tq=tq, tsup=tq, dw=tq)
    mk, mv = _mem_kv(mem, lw)
    mk4, mv4 = _token_major(mk), _token_major(mv)
    o_m = _softmax_attn(pr['mq'], _head_major(mk4).astype(BF16), _augment_v(_head_major(mv4).astype(BF16)),
                        hb=HEADS, tq=tq, tsup=mk.shape[1], causal=False)
    y = _merge(x, o_a, o_b, o_c, o_m, pr['gates'], lw, tm)
    keep = min(A_WIN, s)
    state = (_token_major(pr['ak_s'][:, s - keep:]), _token_major(pr['av_s'][:, s - keep:]), pr['ckv'], pr['kr'],
             _token_major(pr['ck_s']), _token_major(pr['cv_s']), mk4, mv4)
    return y, state


def _layer_step(x, lw, rope_tabs, ca_k, ca_v, cb_ckv, cb_kr, cc_k, cc_v, cm_k, cm_v):
    n, t, _ = x.shape
    n_past = cb_ckv.shape[1]
    assert t == CHUNK and n_past % CHUNK == 0 and ca_k.shape[1] == A_WIN
    dw = 2 * CHUNK
    back = dw - t
    pr = _proj(x, rope_tabs, lw, t)
    ckv_all = jnp.pad(jnp.concatenate([cb_ckv, pr['ckv']], 1), ((0, 0), (0, back), (0, 0)))
    kr_all = jnp.pad(jnp.concatenate([cb_kr, pr['kr']], 1), ((0, 0), (0, back), (0, 0)))
    kb, vb = _kv_expand(ckv_all, kr_all, lw, ckv_all.shape[1])
    o_b = _softmax_attn(pr['bq'], kb, vb, hb=HEADS, tq=t, tsup=n_past, dw=dw, q0=n_past, causal=True)
    frames_last = lambda c: c.transpose(0, 2, 3, 1)
    o_a, o_c, o_m = _step_attn(pr, frames_last(ca_k), frames_last(ca_v), frames_last(cc_k), frames_last(cc_v),
                               frames_last(cm_k), frames_last(cm_v), _band_bias(lw['a_rel_bias'], 1))
    y = _merge(x, o_a, o_b, o_c, o_m, pr['gates'], lw, t)
    state = (jnp.concatenate([ca_k[:, t:], _token_major(pr['ak_s'])], 1),
             jnp.concatenate([ca_v[:, t:], _token_major(pr['av_s'])], 1),
             pr['ckv'], pr['kr'], _token_major(pr['ck_s']), _token_major(pr['cv_s']))
    return y, state


def kernel(x_prompt, x_sample, mem_prompt, cache_a_k, cache_a_v, cache_b_ckv, cache_b_krope, cache_c_k, cache_c_v,
           cache_mem_k, cache_mem_v, norm_g, w_in, a_qn_g, a_kn_g, a_rel_bias, b_cq_g, b_wq_b, b_ckv_g, b_wkv_b,
           b_qn_g, b_qr_g, b_kn_g, b_kr_g, m_norm_g, w_mem_kv, m_qn_g, m_kn_g, out_g, w_out):
    p = dict(norm_g=norm_g, w_in=w_in, a_qn_g=a_qn_g, a_kn_g=a_kn_g, a_rel_bias=a_rel_bias, b_cq_g=b_cq_g,
             b_wq_b=b_wq_b, b_ckv_g=b_ckv_g, b_wkv_b=b_wkv_b, b_qn_g=b_qn_g, b_qr_g=b_qr_g, b_kn_g=b_kn_g,
             b_kr_g=b_kr_g, m_norm_g=m_norm_g, w_mem_kv=w_mem_kv, m_qn_g=m_qn_g, m_kn_g=m_kn_g, out_g=out_g,
             w_out=w_out)
    depth = w_in.shape[0]
    s, t, n_past = x_prompt.shape[1], x_sample.shape[1], cache_b_ckv.shape[2]
    tabs_p = _rope_tables(jnp.arange(s))
    tabs_s = _rope_tables(n_past + jnp.arange(t))
    hp, hs = x_prompt, x_sample
    p_states, s_states = [], []
    for l in range(depth):
        lw = _layer_weights(l, p)
        hp, sp = _layer_prompt(hp, mem_prompt, lw, tabs_p)
        hs, ss = _layer_step(hs, lw, tabs_s, cache_a_k[l], cache_a_v[l], cache_b_ckv[l], cache_b_krope[l],
                             cache_c_k[l], cache_c_v[l], cache_mem_k[l], cache_mem_v[l])
        p_states.append(sp)
        s_states.append(ss)
    outs_p = [jnp.stack([st[i] for st in p_states]) for i in range(8)]
    outs_s = [jnp.stack([st[i] for st in s_states]) for i in range(6)]
    return (hp, hs, *outs_p, *outs_s)
```

```python
import functools

import jax
import jax.numpy as jnp
from jax import lax
from jax.experimental import pallas as pl
from jax.experimental.pallas import tpu as pltpu

F32 = jnp.float32
BF16 = jnp.bfloat16

D_MODEL = 1024
HEAD_DIM = 64
HEADS = 4
GROUP = HEADS * HEAD_DIM
CHUNK = 64
A_PREV_CHUNKS = 8
A_WIN = A_PREV_CHUNKS * CHUNK
REL_CLIP = 256
B_NOPE = 64
B_ROPE = 32
B_QK = B_NOPE + B_ROPE
Q_LORA = 256
KV_LORA = 128
ROPE_THETA = 10000.0
MLA_SCALE = B_QK ** -0.5
HEAD_SCALE = HEAD_DIM ** -0.5
EPS = 1e-6
NEG = -1e30
LOG2E = 1.4426950408889634
SIGN_BIT = -2 ** 31
SUB = 256
PROMPT_TQ = 1024
PROMPT_HB = 2
LANES = 128
VMEM_LIMIT = 48 * 1024 * 1024

OFF_AQ, OFF_AK, OFF_AV, OFF_AG = 0, 256, 512, 768
OFF_BCQ, OFF_BCKV, OFF_BKR, OFF_BG = 1024, 1280, 1408, 1536
OFF_CQ, OFF_CK, OFF_CV, OFF_CG = 1792, 2048, 2304, 2560
OFF_MQ, OFF_MG = 2816, 3072
IN_WIDTH_PADDED = 3328
ROPE_LANE0 = B_NOPE


def _params(n_axes):
    return pltpu.CompilerParams(dimension_semantics=("arbitrary",) * n_axes, vmem_limit_bytes=VMEM_LIMIT)


def _dot(a, b):
    return jnp.dot(a, b, preferred_element_type=F32)


def _dot_nt(a, b):
    return lax.dot_general(a, b, (((1,), (1,)), ((), ())), preferred_element_type=F32)


def _group_mean(x2, g):
    return _dot(x2.astype(BF16), g)


def _rope(x, c, s1, s2):
    return x * c + pltpu.roll(x, LANES - B_ROPE // 2, 1) * s1 + pltpu.roll(x, B_ROPE // 2, 1) * s2


def _proj_kernel(x_ref, c_ref, s1_ref, s2_ref, ng_ref, win_ref, g64_ref, aqg_ref, akg_ref, bcqg_ref, wq_ref,
                 gq_ref, gqg_ref, ckvg_ref, krg_ref, mqg_ref,
                 aq_o, ak_o, av_o, aks_o, avs_o, bq_o, ckv_o, kr_o, cq_o, ck_o, cv_o, cks_o, cvs_o, mq_o, gate_o,
                 scr):
    x = x_ref[0]
    ms = jnp.mean(x * x, axis=-1, keepdims=True)
    xn = (x * lax.rsqrt(ms + EPS) * ng_ref[...]).astype(BF16)
    c, s1, s2 = c_ref[...], s1_ref[...], s2_ref[...]

    def seg(off, width):
        return _dot(xn, win_ref[:, off:off + width])

    def head_rms(h, gain_ref):
        m = _group_mean(h * h, g64_ref[...])
        return h * lax.rsqrt(m + EPS) * gain_ref[...]

    def put_heads(o_ref, val, base=0):
        scr[...] = val
        for h in range(HEADS):
            o_ref[0, base + h] = scr[:, h * HEAD_DIM:(h + 1) * HEAD_DIM].astype(o_ref.dtype)

    put_heads(aq_o, head_rms(seg(OFF_AQ, GROUP), aqg_ref) * HEAD_SCALE)
    ak = head_rms(seg(OFF_AK, GROUP), akg_ref)
    aks_o[0] = ak
    put_heads(ak_o, ak)
    av = seg(OFF_AV, GROUP)
    avs_o[0] = av
    put_heads(av_o, av)

    for gi, off in enumerate((OFF_AG, OFF_BG, OFF_CG, OFF_MG)):
        g = seg(off, GROUP)
        put_heads(gate_o, g * jax.nn.sigmoid(g), base=HEADS * gi)

    h = seg(OFF_BCQ, Q_LORA)
    cq = (h * lax.rsqrt(jnp.mean(h * h, axis=-1, keepdims=True) + EPS) * bcqg_ref[...]).astype(BF16)
    qb = _dot(cq, wq_ref[...])
    for half in range(2):
        part = qb[:, half * 2 * LANES:(half + 1) * 2 * LANES]
        m = _group_mean(part * part, gq_ref[...])
        normed = part * lax.rsqrt(m + EPS) * gqg_ref[:, half * 2 * LANES:(half + 1) * 2 * LANES]
        for hh in range(2):
            xh = normed[:, hh * LANES:(hh + 1) * LANES]
            bq_o[0, 2 * half + hh] = (_rope(xh, c, s1, s2) * (MLA_SCALE * LOG2E)).astype(BF16)

    h = seg(OFF_BCKV, KV_LORA)
    ckv_o[0] = h * lax.rsqrt(jnp.mean(h * h, axis=-1, keepdims=True) + EPS) * ckvg_ref[...]
    h = seg(OFF_BKR, LANES)
    ms = jnp.sum(h * h, axis=-1, keepdims=True) * (1.0 / B_ROPE)
    kr = _rope(h * lax.rsqrt(ms + EPS) * krg_ref[...], c, s1, s2)
    scr[:, 0:LANES] = kr
    kr_o[0] = scr[:, ROPE_LANE0:ROPE_LANE0 + B_ROPE]

    put_heads(cq_o, seg(OFF_CQ, GROUP) * (-HEAD_SCALE * LOG2E))
    ck = seg(OFF_CK, GROUP)
    cks_o[0] = ck
    put_heads(ck_o, ck)
    cv = seg(OFF_CV, GROUP)
    cvs_o[0] = cv
    put_heads(cv_o, cv)

    put_heads(mq_o, head_rms(seg(OFF_MQ, GROUP), mqg_ref) * (HEAD_SCALE * LOG2E))


def _proj(x, rope_tabs, lw, tm):
    n, t, _ = x.shape
    nt = t // tm
    c, s1, s2 = rope_tabs

    def full(a):
        return pl.BlockSpec(a.shape, lambda i, j: (0,) * a.ndim)

    tab_spec = pl.BlockSpec((tm, LANES), lambda i, j: (j, 0))
    weights = (lw['norm_g'], lw['w_in'], lw['g64'], lw['a_qn_g'], lw['a_kn_g'], lw['b_cq_g'], lw['wq'], lw['gq'],
               lw['gq_gain'], lw['b_ckv_g'], lw['kr_gain'], lw['m_qn_g'])

    def hm(width, dtype, heads=HEADS):
        return (jax.ShapeDtypeStruct((n, heads, t, width), dtype),
                pl.BlockSpec((1, heads, tm, width), lambda i, j: (i, 0, j, 0)))

    def tmaj(width):
        return (jax.ShapeDtypeStruct((n, t, width), F32), pl.BlockSpec((1, tm, width), lambda i, j: (i, j, 0)))

    outs = [hm(HEAD_DIM, BF16), hm(HEAD_DIM, BF16), hm(HEAD_DIM, BF16), tmaj(GROUP), tmaj(GROUP),
            hm(LANES, BF16), tmaj(KV_LORA), tmaj(B_ROPE),
            hm(HEAD_DIM, BF16), hm(HEAD_DIM, BF16), hm(HEAD_DIM, BF16), tmaj(GROUP), tmaj(GROUP),
            hm(HEAD_DIM, BF16), hm(HEAD_DIM, F32, 4 * HEADS)]
    names = ('aq', 'ak', 'av', 'ak_s', 'av_s', 'bq', 'ckv', 'kr', 'cq', 'ck', 'cv', 'ck_s', 'cv_s', 'mq', 'gates')
    res = pl.pallas_call(
        _proj_kernel,
        grid=(n, nt),
        in_specs=[pl.BlockSpec((1, tm, D_MODEL), lambda i, j: (i, j, 0)), tab_spec, tab_spec, tab_spec]
        + [full(w) for w in weights],
        out_specs=[o[1] for o in outs],
        out_shape=[o[0] for o in outs],
        scratch_shapes=[pltpu.VMEM((tm, GROUP), F32)],
        compiler_params=_params(2),
        name='proj',
    )(x, c, s1, s2, *weights)
    return dict(zip(names, res))


def _kv_kernel(ckv_ref, kr_ref, wkv_ref, place_ref, gk_ref, gkg_ref, vone_ref, k_o, v_o):
    kv = _dot(ckv_ref[0].astype(BF16), wkv_ref[...])
    kr = _dot(kr_ref[0].astype(BF16), place_ref[...])
    for half in range(2):
        part = kv[:, half * 2 * LANES:(half + 1) * 2 * LANES]
        m = _group_mean(part * part, gk_ref[...])
        normed = part * lax.rsqrt(m + EPS) * gkg_ref[:, half * 2 * LANES:(half + 1) * 2 * LANES]
        for hh in range(2):
            k_o[0, 2 * half + hh] = (normed[:, hh * LANES:(hh + 1) * LANES] + kr).astype(BF16)
    for h in range(HEADS):
        v_o[0, h] = (kv[:, (HEADS + h) * LANES:(HEADS + h + 1) * LANES] + vone_ref[...]).astype(BF16)


def _kv_expand(ckv, kr, lw, tl):
    n, l, _ = ckv.shape
    weights = (lw['wkv'], lw['place'], lw['gk'], lw['gk_gain'], lw['v_one'])
    return pl.pallas_call(
        _kv_kernel,
        grid=(n, l // tl),
        in_specs=[pl.BlockSpec((1, tl, KV_LORA), lambda i, j: (i, j, 0)),
                  pl.BlockSpec((1, tl, B_ROPE), lambda i, j: (i, j, 0))]
        + [pl.BlockSpec(w.shape, lambda i, j: (0, 0)) for w in weights],
        out_specs=[pl.BlockSpec((1, HEADS, tl, LANES), lambda i, j: (i, 0, j, 0)),
                   pl.BlockSpec((1, HEADS, tl, LANES), lambda i, j: (i, 0, j, 0))],
        out_shape=[jax.ShapeDtypeStruct((n, HEADS, l, LANES), BF16)] * 2,
        compiler_params=_params(2),
        name='kv_expand',
    )(ckv, kr, *weights)


def _memkv_kernel(mem_ref, ng_ref, w_ref, g64_ref, kg_ref, k_o, v_o):
    x = mem_ref[0]
    xn = (x * lax.rsqrt(jnp.mean(x * x, axis=-1, keepdims=True) + EPS) * ng_ref[...]).astype(BF16)
    kv = _dot(xn, w_ref[...])
    k = kv[:, :GROUP]
    m = _group_mean(k * k, g64_ref[...])
    k_o[0] = k * lax.rsqrt(m + EPS) * kg_ref[...]
    v_o[0] = kv[:, GROUP:]


def _mem_kv(mem, lw):
    n, nm, _ = mem.shape
    weights = (lw['m_norm_g'], lw['w_mem_kv'], lw['g64'], lw['m_kn_g'])
    return pl.pallas_call(
        _memkv_kernel,
        grid=(n,),
        in_specs=[pl.BlockSpec((1, nm, D_MODEL), lambda i: (i, 0, 0))]
        + [pl.BlockSpec(w.shape, lambda i: (0, 0)) for w in weights],
        out_specs=[pl.BlockSpec((1, nm, GROUP), lambda i: (i, 0, 0))] * 2,
        out_shape=[jax.ShapeDtypeStruct((n, nm, GROUP), F32)] * 2,
        compiler_params=_params(1),
        name='mem_kv',
    )(mem, *weights)


def _band_kernel(q_ref, k_ref, v_ref, bias_ref, o_ref, *, rows, span, front):
    step = pl.program_id(1)
    start = pl.multiple_of(step * rows, CHUNK)

    def attend(first_key, width):
        heads = range(HEADS)
        keys = pl.ds(first_key, width)
        s = [_dot_nt(q_ref[0, h], k_ref[0, h, keys, :]) + bias_ref[h, :, span - width:] for h in heads]
        p = [jnp.exp(s[h] - jnp.max(s[h], axis=-1, keepdims=True)) for h in heads]
        pv = [_dot(p[h].astype(BF16), v_ref[0, h, keys, :]) for h in heads]
        for h in heads:
            o_ref[0, h] = pv[h] / jnp.sum(p[h], axis=-1, keepdims=True)

    early = (A_WIN - front) // rows
    for e in range(early):
        pl.when(step == e)(functools.partial(attend, 0, front + (e + 1) * rows))
    pl.when(step >= early)(lambda: attend(pl.multiple_of(start + front - A_WIN, CHUNK), span))


def _band_attn(q, k, v, bias, g, front):
    n, _, t, _ = q.shape
    rows, span = CHUNK * g, CHUNK * (g + A_PREV_CHUNKS)
    lk = k.shape[2]
    assert (A_WIN - front) % rows == 0 and lk == front + t
    resident = dict(pipeline_mode=pl.Buffered(1)) if t // rows > 1 else {}
    return pl.pallas_call(
        functools.partial(_band_kernel, rows=rows, span=span, front=front),
        grid=(n, t // rows),
        in_specs=[pl.BlockSpec((1, HEADS, rows, HEAD_DIM), lambda i, j: (i, 0, j, 0)),
                  pl.BlockSpec((1, HEADS, lk, HEAD_DIM), lambda i, j: (i, 0, 0, 0), **resident),
                  pl.BlockSpec((1, HEADS, lk, HEAD_DIM), lambda i, j: (i, 0, 0, 0), **resident),
                  pl.BlockSpec((HEADS, rows, span), lambda i, j: (0, 0, 0))],
        out_specs=pl.BlockSpec((1, HEADS, rows, HEAD_DIM), lambda i, j: (i, 0, j, 0)),
        out_shape=jax.ShapeDtypeStruct((n, HEADS, t, HEAD_DIM), F32),
        compiler_params=_params(2),
        name='band_attn',
    )(q, k, v, bias)


def _band_bias(tab, g):
    rows, span = CHUNK * g, CHUNK * (g + A_PREV_CHUNKS)
    i = jnp.arange(rows)[:, None]
    j = jnp.arange(span)[None, :]
    lo = (i // CHUNK) * CHUNK
    inband = (j >= lo) & (j < lo + CHUNK * (A_PREV_CHUNKS + 1))
    period = rows + span - 1
    u = jnp.concatenate([jnp.arange(span), jnp.arange(-(rows - 1), 0)])
    line = tab[:, jnp.clip(A_WIN - u, -REL_CLIP, REL_CLIP) + REL_CLIP].astype(F32)
    skew = jnp.tile(line, (1, rows))[:, :rows * (period - 1)].reshape(-1, rows, period - 1)
    return jnp.where(inband[None], skew[:, :, :span], NEG)


def _resident_spec(block, nq):
    mode = dict(pipeline_mode=pl.Buffered(1)) if nq > 1 else {}
    return pl.BlockSpec(block, lambda i, h, j: (i, h, 0, 0), **mode)


def _softmax_kernel(q_ref, k_ref, v_ref, o_ref, *, hb, tq, tsup, dw, q0, causal, lk):
    q_start = pl.multiple_of(q0 + pl.program_id(2) * tq, CHUNK)

    def score_stage(h, off, n, width):
        q = q_ref[0, h]
        return tuple(_dot_nt(q, k_ref[0, h, pl.ds(pl.multiple_of(off + j * width, CHUNK), width), :])
                     for j in range(n))

    def chunk_mask(scores, width):
        qchunk = lax.broadcasted_iota(jnp.int32, (tq, width), 0) // CHUNK
        return tuple(
            jnp.where((j * width + lax.broadcasted_iota(jnp.int32, (tq, width), 1)) // CHUNK <= qchunk, s, NEG)
            for j, s in enumerate(scores))

    def value_stage(h, off, width, scores, m, acc):
        m_new = m
        for s in scores:
            m_new = jnp.maximum(m_new, jnp.max(s, axis=-1, keepdims=True))
        pv = None
        for j, s in enumerate(scores):
            v = v_ref[0, h, pl.ds(pl.multiple_of(off + j * width, CHUNK), width), :]
            d = _dot(jnp.exp2(s - m_new).astype(BF16), v)
            pv = d if pv is None else pv + d
        return m_new, jnp.exp2(m - m_new) * acc + pv

    heads = range(hb)
    sub = min(tsup, SUB)
    ms = tuple(jnp.full((tq, 1), NEG, F32) for _ in heads)
    accs = tuple(jnp.zeros((tq, LANES), F32) for _ in heads)
    def sweep(i, carry):
        ms, accs = carry
        off = pl.multiple_of(i * tsup, tsup)
        scores = [score_stage(h, off, tsup // sub, sub) for h in heads]
        out = [value_stage(h, off, sub, scores[h], ms[h], accs[h]) for h in heads]
        return tuple(o[0] for o in out), tuple(o[1] for o in out)

    ms, accs = lax.fori_loop(0, q_start // tsup if causal else lk // tsup, sweep, (ms, accs))
    if causal:
        dsub = min(dw, SUB)
        scores = [chunk_mask(score_stage(h, q_start, dw // dsub, dsub), dsub) for h in heads]
        accs = [value_stage(h, q_start, dsub, scores[h], ms[h], accs[h])[1] for h in heads]
    ones_lane = lax.broadcasted_iota(jnp.int32, (tq, LANES), 1) == HEAD_DIM
    for h in heads:
        acc = accs[h]
        denom = jnp.sum(jnp.where(ones_lane, acc, 0.0), axis=-1, keepdims=True)
        o_ref[0, h] = (acc * (1.0 / denom))[:, :HEAD_DIM]


def _softmax_attn(q, k, v, *, hb, tq, tsup, dw=0, q0=0, causal):
    n, _, t, dk = q.shape
    lk = k.shape[2]
    nq = t // tq
    assert t % tq == 0 and tq % CHUNK == 0 and q0 % CHUNK == 0 and HEADS % hb == 0
    if causal:
        assert q0 % tsup == 0 and (nq == 1 or tq % tsup == 0) and q0 + (nq - 1) * tq + dw <= lk
    else:
        assert lk % tsup == 0
    return pl.pallas_call(
        functools.partial(_softmax_kernel, hb=hb, tq=tq, tsup=tsup, dw=dw, q0=q0, causal=causal, lk=lk),
        grid=(n, HEADS // hb, nq),
        in_specs=[pl.BlockSpec((1, hb, tq, dk), lambda i, h, j: (i, h, j, 0)),
                  _resident_spec((1, hb, lk, dk), nq), _resident_spec((1, hb, lk, LANES), nq)],
        out_specs=pl.BlockSpec((1, hb, tq, HEAD_DIM), lambda i, h, j: (i, h, j, 0)),
        out_shape=jax.ShapeDtypeStruct((n, HEADS, t, HEAD_DIM), F32),
        compiler_params=_params(3),
        name='softmax_attn',
    )(q, k, v)


def _augment_v(v):
    n, h, l, _ = v.shape
    return jnp.concatenate([v, jnp.ones((n, h, l, 1), v.dtype), jnp.zeros((n, h, l, LANES - HEAD_DIM - 1), v.dtype)],
                           axis=-1)


def _stick_kernel(q_ref, k_ref, v_ref, tri_ref, o_ref, *, hb, tq, tsup, dw, q0):
    q_start = pl.multiple_of(q0 + pl.program_id(2) * tq, CHUNK)

    def score_stage(h, off, n, width, masked):
        tri = tri_ref[...] if width == SUB else tri_ref[0:width, 0:width]
        q = q_ref[0, h]
        parts = []
        for j in range(n):
            k = k_ref[0, h, pl.ds(pl.multiple_of(off + j * width, CHUNK), width), :]
            nz = _dot_nt(q, k)
            neg_abs = lax.bitcast_convert_type(lax.bitcast_convert_type(nz, jnp.int32) | SIGN_BIT, F32)
            log_1m = jnp.minimum(nz, 0.0) - jnp.log(1.0 + jnp.exp2(neg_abs)) * LOG2E
            if masked:
                mask = (j * width + lax.broadcasted_iota(jnp.int32, (tq, width), 1)
                        < lax.broadcasted_iota(jnp.int32, (tq, width), 0))
                log_1m = jnp.where(mask, log_1m, 0.0)
            tail = _dot(log_1m.astype(BF16), tri)
            expo = log_1m - nz + tail
            if masked:
                expo = jnp.where(mask, expo, NEG)
            parts.append((expo, jnp.sum(log_1m, axis=-1, keepdims=True)))
        return tuple(parts)

    def value_stage(h, off, width, parts, run, acc):
        for j in reversed(range(len(parts))):
            expo, row_sum = parts[j]
            v = v_ref[0, h, pl.ds(pl.multiple_of(off + j * width, CHUNK), width), :]
            acc = acc + _dot(jnp.exp2(expo).astype(BF16), v) * jnp.exp2(run)
            run = run + row_sum
        return run, acc

    heads = range(hb)
    dsub = min(dw, SUB)
    runs = tuple(jnp.zeros((tq, 1), F32) for _ in heads)
    accs = tuple(jnp.zeros((tq, HEAD_DIM), F32) for _ in heads)
    diag = tuple(score_stage(h, q_start, dw // dsub, dsub, True) for h in heads)
    n_super = q_start // tsup

    out = [value_stage(h, q_start, dsub, diag[h], runs[h], accs[h]) for h in heads]
    runs, accs = tuple(o[0] for o in out), tuple(o[1] for o in out)

    def sweep(i, carry):
        runs, accs = carry
        off = pl.multiple_of((n_super - 1 - i) * tsup, tsup)
        out = [value_stage(h, off, SUB, score_stage(h, off, tsup // SUB, SUB, False), runs[h], accs[h])
               for h in heads]
        return tuple(o[0] for o in out), tuple(o[1] for o in out)

    _, accs = lax.fori_loop(0, n_super, sweep, (runs, accs))
    for h in heads:
        o_ref[0, h] = accs[h]


def _stick_attn(q, k, v, *, hb, tq, tsup, dw, q0=0):
    n, _, t, _ = q.shape
    lk = k.shape[2]
    nq = t // tq
    assert t % tq == 0 and q0 % CHUNK == 0 and tq % CHUNK == 0 and tsup % SUB == 0 and HEADS % hb == 0
    assert q0 % tsup == 0 and (nq == 1 or tq % tsup == 0) and q0 + (nq - 1) * tq + dw <= lk
    r = jnp.arange(SUB)
    tri = (r[:, None] > r[None, :]).astype(BF16)
    return pl.pallas_call(
        functools.partial(_stick_kernel, hb=hb, tq=tq, tsup=tsup, dw=dw, q0=q0),
        grid=(n, HEADS // hb, nq),
        in_specs=[pl.BlockSpec((1, hb, tq, HEAD_DIM), lambda i, h, j: (i, h, j, 0)),
                  _resident_spec((1, hb, lk, HEAD_DIM), nq), _resident_spec((1, hb, lk, HEAD_DIM), nq),
                  pl.BlockSpec((SUB, SUB), lambda i, h, j: (0, 0))],
        out_specs=pl.BlockSpec((1, hb, tq, HEAD_DIM), lambda i, h, j: (i, h, j, 0)),
        out_shape=jax.ShapeDtypeStruct((n, HEADS, t, HEAD_DIM), F32),
        compiler_params=_params(3),
        name='stick_attn',
    )(q, k, v, tri)


def _step_attn_kernel(aq_ref, cq_ref, mq_ref, ak_t_ref, av_t_ref, ak_n_ref, av_n_ref, ck_t_ref, cv_t_ref,
                      ck_n_ref, cv_n_ref, mk_t_ref, mv_t_ref, bias_ref, tri_ref, oa_ref, oc_ref, om_ref, *, t):
    n_a, n_c = ak_t_ref.shape[-1], ck_t_ref.shape[-1]
    before = (lax.broadcasted_iota(jnp.int32, (t, t), 1) < lax.broadcasted_iota(jnp.int32, (t, t), 0))
    heads = range(HEADS)
    subs = range(n_c // SUB)

    def log_one_minus(nz):
        neg_abs = lax.bitcast_convert_type(lax.bitcast_convert_type(nz, jnp.int32) | SIGN_BIT, F32)
        return jnp.minimum(nz, 0.0) - jnp.log(1.0 + jnp.exp2(neg_abs)) * LOG2E

    a_c = [_dot(aq_ref[0, h], ak_t_ref[0, h].astype(BF16)) + bias_ref[h, :, :n_a] for h in heads]
    a_n = [_dot_nt(aq_ref[0, h], ak_n_ref[0, h]) + bias_ref[h, :, n_a:] for h in heads]
    m_s = [_dot(mq_ref[0, h], mk_t_ref[0, h].astype(BF16)) for h in heads]
    c_n = [_dot_nt(cq_ref[0, h], ck_n_ref[0, h]) for h in heads]
    c_c = [[_dot(cq_ref[0, h], ck_t_ref[0, h, :, j * SUB:(j + 1) * SUB].astype(BF16)) for j in subs] for h in heads]
    a_m = [jnp.maximum(jnp.max(a_c[h], axis=-1, keepdims=True), jnp.max(a_n[h], axis=-1, keepdims=True))
           for h in heads]
    a_pc = [jnp.exp(a_c[h] - a_m[h]) for h in heads]
    a_pn = [jnp.exp(a_n[h] - a_m[h]) for h in heads]
    m_p = [jnp.exp2(m_s[h] - jnp.max(m_s[h], axis=-1, keepdims=True)) for h in heads]
    l_n = [jnp.where(before, log_one_minus(c_n[h]), 0.0) for h in heads]
    l_c = [[log_one_minus(c_c[h][j]) for j in subs] for h in heads]
    for h in heads:
        pv = (_dot_nt(a_pc[h].astype(BF16), av_t_ref[0, h].astype(BF16))
              + _dot(a_pn[h].astype(BF16), av_n_ref[0, h]))
        oa_ref[0, h] = pv / (jnp.sum(a_pc[h], axis=-1, keepdims=True) + jnp.sum(a_pn[h], axis=-1, keepdims=True))
        om_ref[0, h] = (_dot_nt(m_p[h].astype(BF16), mv_t_ref[0, h].astype(BF16))
                        / jnp.sum(m_p[h], axis=-1, keepdims=True))
    t_n = [_dot(l_n[h].astype(BF16), tri_ref[0:t, 0:t]) for h in heads]
    t_c = [[_dot(l_c[h][j].astype(BF16), tri_ref[...]) for j in subs] for h in heads]
    w_n = [jnp.exp2(jnp.where(before, l_n[h] - c_n[h] + t_n[h], NEG)).astype(BF16) for h in heads]
    w_c = [[jnp.exp2(l_c[h][j] - c_c[h][j] + t_c[h][j]).astype(BF16) for j in subs] for h in heads]
    pv_n = [_dot(w_n[h], cv_n_ref[0, h]) for h in heads]
    pv_c = [[_dot_nt(w_c[h][j], cv_t_ref[0, h, :, j * SUB:(j + 1) * SUB].astype(BF16)) for j in subs]
            for h in heads]
    for h in heads:
        acc = pv_n[h]
        run = jnp.sum(l_n[h], axis=-1, keepdims=True)
        for j in reversed(subs):
            acc = acc + pv_c[h][j] * jnp.exp2(run)
            run = run + jnp.sum(l_c[h][j], axis=-1, keepdims=True)
        oc_ref[0, h] = acc


def _step_attn(pr, layer, ak_t, av_t, ck_t, cv_t, mk_t, mv_t, bias):
    n, _, t, _ = pr['aq'].shape
    assert ck_t.shape[-1] % SUB == 0
    r = jnp.arange(SUB)
    tri = (r[:, None] > r[None, :]).astype(BF16)
    ops = (pr['aq'], pr['cq'], pr['mq'], ak_t, av_t, pr['ak'], pr['av'], ck_t, cv_t, pr['ck'], pr['cv'], mk_t, mv_t)

    def row_spec(a):
        if len(a.shape) == 5:
            return pl.BlockSpec((None, 1) + a.shape[2:], lambda i: (layer, i, 0, 0, 0))
        return pl.BlockSpec((1,) + a.shape[1:], lambda i: (i, 0, 0, 0))

    out = jax.ShapeDtypeStruct((n, HEADS, t, HEAD_DIM), F32)
    return pl.pallas_call(
        functools.partial(_step_attn_kernel, t=t),
        grid=(n,),
        in_specs=[row_spec(a) for a in ops]
        + [pl.BlockSpec(bias.shape, lambda i: (0, 0, 0)), pl.BlockSpec((SUB, SUB), lambda i: (0, 0))],
        out_specs=[row_spec(out)] * 3,
        out_shape=[out] * 3,
        compiler_params=_params(1),
        name='step_attn',
    )(*ops, bias, tri)


def _merge_kernel(x_ref, oa_ref, ob_ref, oc_ref, om_ref, gate_ref, og_ref, w_ref, y_ref):
    acc = x_ref[0]
    for gi, o_ref in enumerate((oa_ref, ob_ref, oc_ref, om_ref)):
        for h in range(HEADS):
            j = HEADS * gi + h
            o = o_ref[0, h]
            y = o * lax.rsqrt(jnp.mean(o * o, axis=-1, keepdims=True) + EPS) * og_ref[j] * gate_ref[0, j]
            acc = acc + _dot(y.astype(BF16), w_ref[j])
    y_ref[0] = acc


def _merge(x, o_a, o_b, o_c, o_m, gates, lw, tm):
    n, t, _ = x.shape
    hspec = pl.BlockSpec((1, HEADS, tm, HEAD_DIM), lambda i, j: (i, 0, j, 0))
    return pl.pallas_call(
        _merge_kernel,
        grid=(n, t // tm),
        in_specs=[pl.BlockSpec((1, tm, D_MODEL), lambda i, j: (i, j, 0)), hspec, hspec, hspec, hspec,
                  pl.BlockSpec((1, 4 * HEADS, tm, HEAD_DIM), lambda i, j: (i, 0, j, 0)),
                  pl.BlockSpec((4 * HEADS, 1, HEAD_DIM), lambda i, j: (0, 0, 0)),
                  pl.BlockSpec((4 * HEADS, HEAD_DIM, D_MODEL), lambda i, j: (0, 0, 0))],
        out_specs=pl.BlockSpec((1, tm, D_MODEL), lambda i, j: (i, j, 0)),
        out_shape=jax.ShapeDtypeStruct((n, t, D_MODEL), F32),
        compiler_params=_params(2),
        name='merge',
    )(x, o_a, o_b, o_c, o_m, gates, lw['out_g'], lw['w_out'])


def _block_diag(block, reps):
    return jnp.kron(jnp.eye(reps, dtype=F32), block)


def _layer_weights(l, p):
    w_in = p['w_in'][l]
    zeros = lambda c: jnp.zeros((D_MODEL, c), F32)
    kr0 = 1408
    w_in = jnp.concatenate([w_in[:, :kr0], zeros(ROPE_LANE0), w_in[:, kr0:kr0 + B_ROPE],
                            zeros(LANES - ROPE_LANE0 - B_ROPE), w_in[:, kr0 + B_ROPE:]], axis=1)
    assert w_in.shape[1] == IN_WIDTH_PADDED
    ones = lambda k: jnp.full((k, k), 1.0 / k, F32)
    zpad = lambda k: jnp.zeros((k, k), F32)
    tile = lambda g, reps=HEADS: jnp.tile(g, reps)[None, :]
    pad_to = lambda g, k: jnp.concatenate([g, jnp.zeros((k - g.shape[0],), F32)])
    wq = jnp.pad(p['b_wq_b'][l].reshape(Q_LORA, HEADS, B_QK), ((0, 0), (0, 0), (0, LANES - B_QK)))
    wkv = p['b_wkv_b'][l].reshape(KV_LORA, HEADS, B_NOPE + HEAD_DIM)
    wk = jnp.pad(wkv[:, :, :B_NOPE], ((0, 0), (0, 0), (0, LANES - B_NOPE))).reshape(KV_LORA, HEADS * LANES)
    wv = jnp.pad(wkv[:, :, B_NOPE:], ((0, 0), (0, 0), (0, LANES - HEAD_DIM))).reshape(KV_LORA, HEADS * LANES)
    gq128 = jax.scipy.linalg.block_diag(ones(B_NOPE), ones(B_ROPE), zpad(LANES - B_QK))
    gk128 = jax.scipy.linalg.block_diag(ones(B_NOPE), zpad(LANES - B_NOPE))
    place = jnp.zeros((B_ROPE, LANES), F32).at[jnp.arange(B_ROPE), ROPE_LANE0 + jnp.arange(B_ROPE)].set(1.0)
    return dict(
        norm_g=p['norm_g'][l][None, :],
        w_in=w_in.astype(BF16),
        g64=_block_diag(ones(HEAD_DIM), HEADS).astype(BF16),
        a_qn_g=tile(p['a_qn_g'][l]), a_kn_g=tile(p['a_kn_g'][l]),
        b_cq_g=p['b_cq_g'][l][None, :],
        wq=wq.reshape(Q_LORA, HEADS * LANES).astype(BF16),
        gq=_block_diag(gq128, 2).astype(BF16),
        gq_gain=tile(jnp.concatenate([p['b_qn_g'][l], pad_to(p['b_qr_g'][l], LANES - B_NOPE)])),
        b_ckv_g=p['b_ckv_g'][l][None, :],
        kr_gain=jnp.concatenate([jnp.zeros((ROPE_LANE0,), F32), pad_to(p['b_kr_g'][l], LANES - ROPE_LANE0)])[None, :],
        m_qn_g=tile(p['m_qn_g'][l]),
        wkv=jnp.concatenate([wk, wv], axis=1).astype(BF16),
        place=place.astype(BF16),
        gk=_block_diag(gk128, 2).astype(BF16),
        gk_gain=tile(pad_to(p['b_kn_g'][l], LANES)),
        v_one=jnp.zeros((1, LANES), F32).at[0, HEAD_DIM].set(1.0),
        m_norm_g=p['m_norm_g'][l][None, :],
        w_mem_kv=p['w_mem_kv'][l].astype(BF16),
        m_kn_g=tile(p['m_kn_g'][l]),
        out_g=p['out_g'][l].reshape(4 * HEADS, 1, HEAD_DIM),
        w_out=p['w_out'][l].reshape(4 * HEADS, HEAD_DIM, D_MODEL).astype(BF16),
        a_rel_bias=p['a_rel_bias'][l],
    )


def _rope_tables(pos):
    half = B_ROPE // 2
    freqs = ROPE_THETA ** (-jnp.arange(half, dtype=F32) / half)
    ang = pos.astype(F32)[:, None] * freqs[None, :]
    cos, sin = jnp.cos(ang), jnp.sin(ang)
    t = pos.shape[0]
    one, zero = jnp.ones((t, ROPE_LANE0), F32), jnp.zeros((t, ROPE_LANE0), F32)
    tail, z16 = jnp.zeros((t, LANES - ROPE_LANE0 - B_ROPE), F32), jnp.zeros((t, half), F32)
    c = jnp.concatenate([one, cos, cos, tail], axis=1)
    s1 = jnp.concatenate([zero, -sin, z16, tail], axis=1)
    s2 = jnp.concatenate([zero, z16, sin, tail], axis=1)
    return c, s1, s2


def _head_major(a):
    return a.transpose(0, 2, 1, 3)


def _token_major(a):
    return a.reshape(a.shape[0], a.shape[1], HEADS, HEAD_DIM)


def _pad_rows(a, front, back):
    return jnp.pad(a, ((0, 0), (0, 0), (front, back), (0, 0)))


def _tiles(s):
    tm = min(512, s)
    tq = min(PROMPT_TQ, s)
    g = min(4, s // CHUNK)
    return tm, tq, g


def _layer_prompt(x, mem, lw, rope_tabs):
    n, s, _ = x.shape
    tm, tq, g = _tiles(s)
    pr = _proj(x, rope_tabs, lw, tm)
    kb, vb = _kv_expand(pr['ckv'], pr['kr'], lw, min(1024, s))
    bias = _band_bias(lw['a_rel_bias'], g)
    o_a = _band_attn(pr['aq'], pr['ak'], pr['av'], bias, g, 0)
    o_b = _softmax_attn(pr['bq'], kb, vb, hb=PROMPT_HB, tq=tq, tsup=tq, dw=tq, causal=True)
    o_c = _stick_attn(pr['cq'], pr['ck'], pr['cv'], hb=PROMPT_HB, tq=tq, tsup=tq, dw=tq)
    mk, mv = _mem_kv(mem, lw)
    mk4, mv4 = _token_major(mk), _token_major(mv)
    o_m = _softmax_attn(pr['mq'], _head_major(mk4).astype(BF16), _augment_v(_head_major(mv4).astype(BF16)),
                        hb=HEADS, tq=tq, tsup=mk.shape[1], causal=False)
    y = _merge(x, o_a, o_b, o_c, o_m, pr['gates'], lw, tm)
    keep = min(A_WIN, s)
    state = (_token_major(pr['ak_s'][:, s - keep:]), _token_major(pr['av_s'][:, s - keep:]), pr['ckv'], pr['kr'],
             _token_major(pr['ck_s']), _token_major(pr['cv_s']), mk4, mv4)
    return y, state


def _layer_step(x, lw, rope_tabs, layer, cb_ckv, cb_kr, caches_t):
    n, t, _ = x.shape
    n_past = cb_ckv.shape[1]
    assert t == CHUNK and n_past % CHUNK == 0 and caches_t[0].shape[-1] == A_WIN
    dw = 2 * CHUNK
    back = dw - t
    pr = _proj(x, rope_tabs, lw, t)
    ckv_all = jnp.pad(jnp.concatenate([cb_ckv, pr['ckv']], 1), ((0, 0), (0, back), (0, 0)))
    kr_all = jnp.pad(jnp.concatenate([cb_kr, pr['kr']], 1), ((0, 0), (0, back), (0, 0)))
    kb, vb = _kv_expand(ckv_all, kr_all, lw, ckv_all.shape[1])
    o_b = _softmax_attn(pr['bq'], kb, vb, hb=HEADS, tq=t, tsup=n_past, dw=dw, q0=n_past, causal=True)
    o_a, o_c, o_m = _step_attn(pr, layer, *caches_t, _band_bias(lw['a_rel_bias'], 1))
    y = _merge(x, o_a, o_b, o_c, o_m, pr['gates'], lw, t)
    state = (_token_major(pr['ak_s']), _token_major(pr['av_s']), pr['ckv'], pr['kr'],
             _token_major(pr['ck_s']), _token_major(pr['cv_s']))
    return y, state


def kernel(x_prompt, x_sample, mem_prompt, cache_a_k, cache_a_v, cache_b_ckv, cache_b_krope, cache_c_k, cache_c_v,
           cache_mem_k, cache_mem_v, norm_g, w_in, a_qn_g, a_kn_g, a_rel_bias, b_cq_g, b_wq_b, b_ckv_g, b_wkv_b,
           b_qn_g, b_qr_g, b_kn_g, b_kr_g, m_norm_g, w_mem_kv, m_qn_g, m_kn_g, out_g, w_out):
    p = dict(norm_g=norm_g, w_in=w_in, a_qn_g=a_qn_g, a_kn_g=a_kn_g, a_rel_bias=a_rel_bias, b_cq_g=b_cq_g,
             b_wq_b=b_wq_b, b_ckv_g=b_ckv_g, b_wkv_b=b_wkv_b, b_qn_g=b_qn_g, b_qr_g=b_qr_g, b_kn_g=b_kn_g,
             b_kr_g=b_kr_g, m_norm_g=m_norm_g, w_mem_kv=w_mem_kv, m_qn_g=m_qn_g, m_kn_g=m_kn_g, out_g=out_g,
             w_out=w_out)
    depth = w_in.shape[0]
    s, t, n_past = x_prompt.shape[1], x_sample.shape[1], cache_b_ckv.shape[2]
    tabs_p = _rope_tables(jnp.arange(s))
    tabs_s = _rope_tables(n_past + jnp.arange(t))
    hp, hs = x_prompt, x_sample
    caches_t = tuple(c.transpose(0, 1, 3, 4, 2)
                     for c in (cache_a_k, cache_a_v, cache_c_k, cache_c_v, cache_mem_k, cache_mem_v))
    p_states, s_states = [], []
    for l in range(depth):
        lw = _layer_weights(l, p)
        hp, sp = _layer_prompt(hp, mem_prompt, lw, tabs_p)
        hs, ss = _layer_step(hs, lw, tabs_s, l, cache_b_ckv[l], cache_b_krope[l], caches_t)
        p_states.append(sp)
        s_states.append(ss)
    outs_p = [jnp.stack([st[i] for st in p_states]) for i in range(8)]
    outs_s = [jnp.stack([st[i] for st in s_states]) for i in range(6)]
    outs_s[0] = jnp.concatenate([cache_a_k[:, :, t:], outs_s[0]], axis=2)
    outs_s[1] = jnp.concatenate([cache_a_v[:, :, t:], outs_s[1]], axis=2)
    return (hp, hs, *outs_p, *outs_s)
```

```python
import functools

import jax
import jax.numpy as jnp
from jax import lax
from jax.experimental import pallas as pl
from jax.experimental.pallas import tpu as pltpu

F32 = jnp.float32
BF16 = jnp.bfloat16

D_MODEL = 1024
HEAD_DIM = 64
HEADS = 4
GROUP = HEADS * HEAD_DIM
CHUNK = 64
A_PREV_CHUNKS = 8
A_WIN = A_PREV_CHUNKS * CHUNK
REL_CLIP = 256
B_NOPE = 64
B_ROPE = 32
B_QK = B_NOPE + B_ROPE
Q_LORA = 256
KV_LORA = 128
ROPE_THETA = 10000.0
MLA_SCALE = B_QK ** -0.5
HEAD_SCALE = HEAD_DIM ** -0.5
EPS = 1e-6
NEG = -1e30
LOG2E = 1.4426950408889634
SIGN_BIT = -2 ** 31
SUB = 256
PROMPT_TQ = 1024
PROMPT_HB = 2
LANES = 128
VMEM_LIMIT = 48 * 1024 * 1024

OFF_AQ, OFF_AK, OFF_AV, OFF_AG = 0, 256, 512, 768
OFF_BCQ, OFF_BCKV, OFF_BKR, OFF_BG = 1024, 1280, 1408, 1536
OFF_CQ, OFF_CK, OFF_CV, OFF_CG = 1792, 2048, 2304, 2560
OFF_MQ, OFF_MG = 2816, 3072
IN_WIDTH_PADDED = 3328
ROPE_LANE0 = B_NOPE


def _params(n_axes):
    return pltpu.CompilerParams(dimension_semantics=("arbitrary",) * n_axes, vmem_limit_bytes=VMEM_LIMIT)


def _dot(a, b):
    return jnp.dot(a, b, preferred_element_type=F32)


def _dot_nt(a, b):
    return lax.dot_general(a, b, (((1,), (1,)), ((), ())), preferred_element_type=F32)


def _head_rms(o):
    return o * lax.rsqrt(jnp.mean(o * o, axis=-1, keepdims=True) + EPS)


def _group_mean(x2, g):
    return _dot(x2.astype(BF16), g)


def _rope(x, c, s1, s2):
    return x * c + pltpu.roll(x, LANES - B_ROPE // 2, 1) * s1 + pltpu.roll(x, B_ROPE // 2, 1) * s2


def _proj_kernel(x_ref, c_ref, s1_ref, s2_ref, ng_ref, win_ref, g64_ref, aqg_ref, akg_ref, bcqg_ref, wq_ref,
                 gq_ref, gqg_ref, ckvg_ref, krg_ref, mqg_ref,
                 aq_o, ak_o, av_o, aks_o, avs_o, bq_o, ckv_o, kr_o, cq_o, ck_o, cv_o, cks_o, cvs_o, mq_o, gate_o,
                 scr):
    x = x_ref[0]
    ms = jnp.mean(x * x, axis=-1, keepdims=True)
    xn = (x * lax.rsqrt(ms + EPS) * ng_ref[...]).astype(BF16)
    c, s1, s2 = c_ref[...], s1_ref[...], s2_ref[...]

    def seg(off, width):
        return _dot(xn, win_ref[:, off:off + width])

    def head_rms(h, gain_ref):
        m = _group_mean(h * h, g64_ref[...])
        return h * lax.rsqrt(m + EPS) * gain_ref[...]

    def put_heads(o_ref, val, base=0):
        scr[...] = val
        for h in range(HEADS):
            o_ref[0, base + h] = scr[:, h * HEAD_DIM:(h + 1) * HEAD_DIM].astype(o_ref.dtype)

    put_heads(aq_o, head_rms(seg(OFF_AQ, GROUP), aqg_ref) * HEAD_SCALE)
    ak = head_rms(seg(OFF_AK, GROUP), akg_ref)
    aks_o[0] = ak
    put_heads(ak_o, ak)
    av = seg(OFF_AV, GROUP)
    avs_o[0] = av
    put_heads(av_o, av)

    for gi, off in enumerate((OFF_AG, OFF_BG, OFF_CG, OFF_MG)):
        g = seg(off, GROUP)
        put_heads(gate_o, g * jax.nn.sigmoid(g), base=HEADS * gi)

    h = seg(OFF_BCQ, Q_LORA)
    cq = (h * lax.rsqrt(jnp.mean(h * h, axis=-1, keepdims=True) + EPS) * bcqg_ref[...]).astype(BF16)
    qb = _dot(cq, wq_ref[...])
    for half in range(2):
        part = qb[:, half * 2 * LANES:(half + 1) * 2 * LANES]
        m = _group_mean(part * part, gq_ref[...])
        normed = part * lax.rsqrt(m + EPS) * gqg_ref[:, half * 2 * LANES:(half + 1) * 2 * LANES]
        for hh in range(2):
            xh = normed[:, hh * LANES:(hh + 1) * LANES]
            bq_o[0, 2 * half + hh] = (_rope(xh, c, s1, s2) * (MLA_SCALE * LOG2E)).astype(BF16)

    h = seg(OFF_BCKV, KV_LORA)
    ckv_o[0] = h * lax.rsqrt(jnp.mean(h * h, axis=-1, keepdims=True) + EPS) * ckvg_ref[...]
    h = seg(OFF_BKR, LANES)
    ms = jnp.sum(h * h, axis=-1, keepdims=True) * (1.0 / B_ROPE)
    kr = _rope(h * lax.rsqrt(ms + EPS) * krg_ref[...], c, s1, s2)
    scr[:, 0:LANES] = kr
    kr_o[0] = scr[:, ROPE_LANE0:ROPE_LANE0 + B_ROPE]

    put_heads(cq_o, seg(OFF_CQ, GROUP) * (-HEAD_SCALE * LOG2E))
    ck = seg(OFF_CK, GROUP)
    cks_o[0] = ck
    put_heads(ck_o, ck)
    cv = seg(OFF_CV, GROUP)
    cvs_o[0] = cv
    put_heads(cv_o, cv)

    put_heads(mq_o, head_rms(seg(OFF_MQ, GROUP), mqg_ref) * (HEAD_SCALE * LOG2E))


def _proj(x, rope_tabs, lw, tm):
    n, t, _ = x.shape
    nt = t // tm
    c, s1, s2 = rope_tabs

    def full(a):
        return pl.BlockSpec(a.shape, lambda i, j: (0,) * a.ndim)

    tab_spec = pl.BlockSpec((tm, LANES), lambda i, j: (j, 0))
    weights = (lw['norm_g'], lw['w_in'], lw['g64'], lw['a_qn_g'], lw['a_kn_g'], lw['b_cq_g'], lw['wq'], lw['gq'],
               lw['gq_gain'], lw['b_ckv_g'], lw['kr_gain'], lw['m_qn_g'])

    def hm(width, dtype, heads=HEADS):
        return (jax.ShapeDtypeStruct((n, heads, t, width), dtype),
                pl.BlockSpec((1, heads, tm, width), lambda i, j: (i, 0, j, 0)))

    def tmaj(width):
        return (jax.ShapeDtypeStruct((n, t, width), F32), pl.BlockSpec((1, tm, width), lambda i, j: (i, j, 0)))

    outs = [hm(HEAD_DIM, BF16), hm(HEAD_DIM, BF16), hm(HEAD_DIM, BF16), tmaj(GROUP), tmaj(GROUP),
            hm(LANES, BF16), tmaj(KV_LORA), tmaj(B_ROPE),
            hm(HEAD_DIM, BF16), hm(HEAD_DIM, BF16), hm(HEAD_DIM, BF16), tmaj(GROUP), tmaj(GROUP),
            hm(HEAD_DIM, BF16), hm(HEAD_DIM, F32, 4 * HEADS)]
    names = ('aq', 'ak', 'av', 'ak_s', 'av_s', 'bq', 'ckv', 'kr', 'cq', 'ck', 'cv', 'ck_s', 'cv_s', 'mq', 'gates')
    res = pl.pallas_call(
        _proj_kernel,
        grid=(n, nt),
        in_specs=[pl.BlockSpec((1, tm, D_MODEL), lambda i, j: (i, j, 0)), tab_spec, tab_spec, tab_spec]
        + [full(w) for w in weights],
        out_specs=[o[1] for o in outs],
        out_shape=[o[0] for o in outs],
        scratch_shapes=[pltpu.VMEM((tm, GROUP), F32)],
        compiler_params=_params(2),
        name='proj',
    )(x, c, s1, s2, *weights)
    return dict(zip(names, res))


def _kv_kernel(ckv_ref, kr_ref, wkv_ref, place_ref, gk_ref, gkg_ref, vone_ref, k_o, v_o):
    kv = _dot(ckv_ref[0].astype(BF16), wkv_ref[...])
    kr = _dot(kr_ref[0].astype(BF16), place_ref[...])
    for half in range(2):
        part = kv[:, half * 2 * LANES:(half + 1) * 2 * LANES]
        m = _group_mean(part * part, gk_ref[...])
        normed = part * lax.rsqrt(m + EPS) * gkg_ref[:, half * 2 * LANES:(half + 1) * 2 * LANES]
        for hh in range(2):
            k_o[0, 2 * half + hh] = (normed[:, hh * LANES:(hh + 1) * LANES] + kr).astype(BF16)
    for h in range(HEADS):
        v_o[0, h] = (kv[:, (HEADS + h) * LANES:(HEADS + h + 1) * LANES] + vone_ref[...]).astype(BF16)


def _kv_expand(ckv, kr, lw, tl):
    n, l, _ = ckv.shape
    weights = (lw['wkv'], lw['place'], lw['gk'], lw['gk_gain'], lw['v_one'])
    return pl.pallas_call(
        _kv_kernel,
        grid=(n, l // tl),
        in_specs=[pl.BlockSpec((1, tl, KV_LORA), lambda i, j: (i, j, 0)),
                  pl.BlockSpec((1, tl, B_ROPE), lambda i, j: (i, j, 0))]
        + [pl.BlockSpec(w.shape, lambda i, j: (0, 0)) for w in weights],
        out_specs=[pl.BlockSpec((1, HEADS, tl, LANES), lambda i, j: (i, 0, j, 0)),
                   pl.BlockSpec((1, HEADS, tl, LANES), lambda i, j: (i, 0, j, 0))],
        out_shape=[jax.ShapeDtypeStruct((n, HEADS, l, LANES), BF16)] * 2,
        compiler_params=_params(2),
        name='kv_expand',
    )(ckv, kr, *weights)


def _memkv_kernel(mem_ref, ng_ref, w_ref, g64_ref, kg_ref, k_o, v_o):
    x = mem_ref[0]
    xn = (x * lax.rsqrt(jnp.mean(x * x, axis=-1, keepdims=True) + EPS) * ng_ref[...]).astype(BF16)
    kv = _dot(xn, w_ref[...])
    k = kv[:, :GROUP]
    m = _group_mean(k * k, g64_ref[...])
    k_o[0] = k * lax.rsqrt(m + EPS) * kg_ref[...]
    v_o[0] = kv[:, GROUP:]


def _mem_kv(mem, lw):
    n, nm, _ = mem.shape
    weights = (lw['m_norm_g'], lw['w_mem_kv'], lw['g64'], lw['m_kn_g'])
    return pl.pallas_call(
        _memkv_kernel,
        grid=(n,),
        in_specs=[pl.BlockSpec((1, nm, D_MODEL), lambda i: (i, 0, 0))]
        + [pl.BlockSpec(w.shape, lambda i: (0, 0)) for w in weights],
        out_specs=[pl.BlockSpec((1, nm, GROUP), lambda i: (i, 0, 0))] * 2,
        out_shape=[jax.ShapeDtypeStruct((n, nm, GROUP), F32)] * 2,
        compiler_params=_params(1),
        name='mem_kv',
    )(mem, *weights)


def _band_kernel(q_ref, k_ref, v_ref, bias_ref, o_ref, *, rows, span, front):
    step = pl.program_id(1)
    start = pl.multiple_of(step * rows, CHUNK)

    def attend(first_key, width):
        heads = range(HEADS)
        keys = pl.ds(first_key, width)
        s = [_dot_nt(q_ref[0, h], k_ref[0, h, keys, :]) + bias_ref[h, :, span - width:] for h in heads]
        p = [jnp.exp(s[h] - jnp.max(s[h], axis=-1, keepdims=True)) for h in heads]
        pv = [_dot(p[h].astype(BF16), v_ref[0, h, keys, :]) for h in heads]
        for h in heads:
            o_ref[0, h] = _head_rms(pv[h] / jnp.sum(p[h], axis=-1, keepdims=True))

    early = (A_WIN - front) // rows
    for e in range(early):
        pl.when(step == e)(functools.partial(attend, 0, front + (e + 1) * rows))
    pl.when(step >= early)(lambda: attend(pl.multiple_of(start + front - A_WIN, CHUNK), span))


def _band_attn(q, k, v, bias, g, front):
    n, _, t, _ = q.shape
    rows, span = CHUNK * g, CHUNK * (g + A_PREV_CHUNKS)
    lk = k.shape[2]
    assert (A_WIN - front) % rows == 0 and lk == front + t
    resident = dict(pipeline_mode=pl.Buffered(1)) if t // rows > 1 else {}
    return pl.pallas_call(
        functools.partial(_band_kernel, rows=rows, span=span, front=front),
        grid=(n, t // rows),
        in_specs=[pl.BlockSpec((1, HEADS, rows, HEAD_DIM), lambda i, j: (i, 0, j, 0)),
                  pl.BlockSpec((1, HEADS, lk, HEAD_DIM), lambda i, j: (i, 0, 0, 0), **resident),
                  pl.BlockSpec((1, HEADS, lk, HEAD_DIM), lambda i, j: (i, 0, 0, 0), **resident),
                  pl.BlockSpec((HEADS, rows, span), lambda i, j: (0, 0, 0))],
        out_specs=pl.BlockSpec((1, HEADS, rows, HEAD_DIM), lambda i, j: (i, 0, j, 0)),
        out_shape=jax.ShapeDtypeStruct((n, HEADS, t, HEAD_DIM), F32),
        compiler_params=_params(2),
        name='band_attn',
    )(q, k, v, bias)


def _band_bias(tab, g):
    rows, span = CHUNK * g, CHUNK * (g + A_PREV_CHUNKS)
    i = jnp.arange(rows)[:, None]
    j = jnp.arange(span)[None, :]
    lo = (i // CHUNK) * CHUNK
    inband = (j >= lo) & (j < lo + CHUNK * (A_PREV_CHUNKS + 1))
    period = rows + span - 1
    u = jnp.concatenate([jnp.arange(span), jnp.arange(-(rows - 1), 0)])
    line = tab[:, jnp.clip(A_WIN - u, -REL_CLIP, REL_CLIP) + REL_CLIP].astype(F32)
    skew = jnp.tile(line, (1, rows))[:, :rows * (period - 1)].reshape(-1, rows, period - 1)
    return jnp.where(inband[None], skew[:, :, :span], NEG)


def _resident_spec(block, nq):
    mode = dict(pipeline_mode=pl.Buffered(1)) if nq > 1 else {}
    return pl.BlockSpec(block, lambda i, h, j: (i, h, 0, 0), **mode)


def _softmax_kernel(q_ref, k_ref, v_ref, o_ref, *, hb, tq, tsup, dw, q0, causal, lk):
    q_start = pl.multiple_of(q0 + pl.program_id(2) * tq, CHUNK)

    def score_stage(h, off, n, width):
        q = q_ref[0, h]
        return tuple(_dot_nt(q, k_ref[0, h, pl.ds(pl.multiple_of(off + j * width, CHUNK), width), :])
                     for j in range(n))

    def chunk_mask(scores, width):
        qchunk = lax.broadcasted_iota(jnp.int32, (tq, width), 0) // CHUNK
        return tuple(
            jnp.where((j * width + lax.broadcasted_iota(jnp.int32, (tq, width), 1)) // CHUNK <= qchunk, s, NEG)
            for j, s in enumerate(scores))

    def value_stage(h, off, width, scores, m, acc):
        m_new = m
        for s in scores:
            m_new = jnp.maximum(m_new, jnp.max(s, axis=-1, keepdims=True))
        pv = None
        for j, s in enumerate(scores):
            v = v_ref[0, h, pl.ds(pl.multiple_of(off + j * width, CHUNK), width), :]
            d = _dot(jnp.exp2(s - m_new).astype(BF16), v)
            pv = d if pv is None else pv + d
        return m_new, jnp.exp2(m - m_new) * acc + pv

    heads = range(hb)
    sub = min(tsup, SUB)
    ms = tuple(jnp.full((tq, 1), NEG, F32) for _ in heads)
    accs = tuple(jnp.zeros((tq, LANES), F32) for _ in heads)
    def sweep(i, carry):
        ms, accs = carry
        off = pl.multiple_of(i * tsup, tsup)
        scores = [score_stage(h, off, tsup // sub, sub) for h in heads]
        out = [value_stage(h, off, sub, scores[h], ms[h], accs[h]) for h in heads]
        return tuple(o[0] for o in out), tuple(o[1] for o in out)

    ms, accs = lax.fori_loop(0, q_start // tsup if causal else lk // tsup, sweep, (ms, accs))
    if causal:
        dsub = min(dw, SUB)
        n_diag = -(-min(dw, tq) // dsub)
        scores = [chunk_mask(score_stage(h, q_start, n_diag, dsub), dsub) for h in heads]
        accs = [value_stage(h, q_start, dsub, scores[h], ms[h], accs[h])[1] for h in heads]
    ones_lane = lax.broadcasted_iota(jnp.int32, (tq, LANES), 1) == HEAD_DIM
    for h in heads:
        acc = accs[h]
        denom = jnp.sum(jnp.where(ones_lane, acc, 0.0), axis=-1, keepdims=True)
        o_ref[0, h] = _head_rms((acc * (1.0 / denom))[:, :HEAD_DIM])


def _softmax_attn(q, k, v, *, hb, tq, tsup, dw=0, q0=0, causal):
    n, _, t, dk = q.shape
    lk = k.shape[2]
    nq = t // tq
    assert t % tq == 0 and tq % CHUNK == 0 and q0 % CHUNK == 0 and HEADS % hb == 0
    if causal:
        assert q0 % tsup == 0 and (nq == 1 or tq % tsup == 0) and q0 + (nq - 1) * tq + dw <= lk
    else:
        assert lk % tsup == 0
    return pl.pallas_call(
        functools.partial(_softmax_kernel, hb=hb, tq=tq, tsup=tsup, dw=dw, q0=q0, causal=causal, lk=lk),
        grid=(n, HEADS // hb, nq),
        in_specs=[pl.BlockSpec((1, hb, tq, dk), lambda i, h, j: (i, h, j, 0)),
                  _resident_spec((1, hb, lk, dk), nq), _resident_spec((1, hb, lk, LANES), nq)],
        out_specs=pl.BlockSpec((1, hb, tq, HEAD_DIM), lambda i, h, j: (i, h, j, 0)),
        out_shape=jax.ShapeDtypeStruct((n, HEADS, t, HEAD_DIM), F32),
        compiler_params=_params(3),
        name='softmax_attn',
    )(q, k, v)


def _augment_v(v):
    n, h, l, _ = v.shape
    return jnp.concatenate([v, jnp.ones((n, h, l, 1), v.dtype), jnp.zeros((n, h, l, LANES - HEAD_DIM - 1), v.dtype)],
                           axis=-1)


def _stick_kernel(q_ref, k_ref, v_ref, tri_ref, o_ref, *, hb, tq, tsup, dw, q0):
    q_start = pl.multiple_of(q0 + pl.program_id(2) * tq, CHUNK)

    def log_one_minus(nz):
        neg_abs = lax.bitcast_convert_type(lax.bitcast_convert_type(nz, jnp.int32) | SIGN_BIT, F32)
        return jnp.minimum(nz, 0.0) - jnp.log(1.0 + jnp.exp2(neg_abs)) * LOG2E

    def score_stage(h, off, n, width):
        q = q_ref[0, h]
        tri = tri_ref[...]
        parts = []
        for j in range(n):
            k = k_ref[0, h, pl.ds(pl.multiple_of(off + j * width, CHUNK), width), :]
            nz = _dot_nt(q, k)
            log_1m = log_one_minus(nz)
            expo = log_1m - nz + _dot(log_1m.astype(BF16), tri)
            parts.append((expo, jnp.sum(log_1m, axis=-1, keepdims=True)))
        return tuple(parts)

    def diagonal(h):
        dsub = min(dw, SUB)
        tri = tri_ref[...] if dsub == SUB else tri_ref[0:dsub, 0:dsub]
        q = q_ref[0, h]
        parts = []
        for j in range(-(-min(dw, tq) // dsub)):
            keys = pl.ds(pl.multiple_of(q_start + j * dsub, CHUNK), dsub)
            nz = _dot_nt(q, k_ref[0, h, keys, :])
            before = (j * dsub + lax.broadcasted_iota(jnp.int32, (tq, dsub), 1)
                      < lax.broadcasted_iota(jnp.int32, (tq, dsub), 0))
            log_1m = jnp.where(before, log_one_minus(nz), 0.0)
            expo = jnp.where(before, log_1m - nz + _dot(log_1m.astype(BF16), tri), NEG)
            parts.append((expo, jnp.sum(log_1m, axis=-1, keepdims=True)))
        return value_stage(h, q_start, dsub, tuple(parts), jnp.zeros((tq, 1), F32), jnp.zeros((tq, HEAD_DIM), F32))

    def value_stage(h, off, width, parts, run, acc):
        for j in reversed(range(len(parts))):
            expo, row_sum = parts[j]
            v = v_ref[0, h, pl.ds(pl.multiple_of(off + j * width, CHUNK), width), :]
            acc = acc + _dot(jnp.exp2(expo).astype(BF16), v) * jnp.exp2(run)
            run = run + row_sum
        return run, acc

    heads = range(hb)
    out = [diagonal(h) for h in heads]
    runs, accs = tuple(o[0] for o in out), tuple(o[1] for o in out)
    n_super = q_start // tsup

    def sweep(i, carry):
        runs, accs = carry
        off = pl.multiple_of((n_super - 1 - i) * tsup, tsup)
        out = [value_stage(h, off, SUB, score_stage(h, off, tsup // SUB, SUB), runs[h], accs[h])
               for h in heads]
        return tuple(o[0] for o in out), tuple(o[1] for o in out)

    _, accs = lax.fori_loop(0, n_super, sweep, (runs, accs))
    for h in heads:
        o_ref[0, h] = _head_rms(accs[h])


def _stick_attn(q, k, v, *, hb, tq, tsup, dw, q0=0):
    n, _, t, _ = q.shape
    lk = k.shape[2]
    nq = t // tq
    assert t % tq == 0 and q0 % CHUNK == 0 and tq % CHUNK == 0 and tsup % SUB == 0 and HEADS % hb == 0
    assert q0 % tsup == 0 and (nq == 1 or tq % tsup == 0) and q0 + (nq - 1) * tq + dw <= lk
    r = jnp.arange(SUB)
    tri = (r[:, None] > r[None, :]).astype(BF16)
    return pl.pallas_call(
        functools.partial(_stick_kernel, hb=hb, tq=tq, tsup=tsup, dw=dw, q0=q0),
        grid=(n, HEADS // hb, nq),
        in_specs=[pl.BlockSpec((1, hb, tq, HEAD_DIM), lambda i, h, j: (i, h, j, 0)),
                  _resident_spec((1, hb, lk, HEAD_DIM), nq), _resident_spec((1, hb, lk, HEAD_DIM), nq),
                  pl.BlockSpec((SUB, SUB), lambda i, h, j: (0, 0))],
        out_specs=pl.BlockSpec((1, hb, tq, HEAD_DIM), lambda i, h, j: (i, h, j, 0)),
        out_shape=jax.ShapeDtypeStruct((n, HEADS, t, HEAD_DIM), F32),
        compiler_params=_params(3),
        name='stick_attn',
    )(q, k, v, tri)


def _step_attn_kernel(aq_ref, cq_ref, mq_ref, ak_t_ref, av_t_ref, ak_n_ref, av_n_ref, ck_t_ref, cv_t_ref,
                      ck_n_ref, cv_n_ref, mk_t_ref, mv_t_ref, bias_ref, tri_ref, oa_ref, oc_ref, om_ref, *, t):
    n_a, n_c = ak_t_ref.shape[-1], ck_t_ref.shape[-1]
    before = (lax.broadcasted_iota(jnp.int32, (t, t), 1) < lax.broadcasted_iota(jnp.int32, (t, t), 0))
    heads = range(HEADS)
    subs = range(n_c // SUB)

    def log_one_minus(nz):
        neg_abs = lax.bitcast_convert_type(lax.bitcast_convert_type(nz, jnp.int32) | SIGN_BIT, F32)
        return jnp.minimum(nz, 0.0) - jnp.log(1.0 + jnp.exp2(neg_abs)) * LOG2E

    a_c = [_dot(aq_ref[0, h], ak_t_ref[0, h].astype(BF16)) + bias_ref[h, :, :n_a] for h in heads]
    a_n = [_dot_nt(aq_ref[0, h], ak_n_ref[0, h]) + bias_ref[h, :, n_a:] for h in heads]
    m_s = [_dot(mq_ref[0, h], mk_t_ref[0, h].astype(BF16)) for h in heads]
    c_n = [_dot_nt(cq_ref[0, h], ck_n_ref[0, h]) for h in heads]
    c_c = [[_dot(cq_ref[0, h], ck_t_ref[0, h, :, j * SUB:(j + 1) * SUB].astype(BF16)) for j in subs] for h in heads]
    a_m = [jnp.maximum(jnp.max(a_c[h], axis=-1, keepdims=True), jnp.max(a_n[h], axis=-1, keepdims=True))
           for h in heads]
    a_pc = [jnp.exp(a_c[h] - a_m[h]) for h in heads]
    a_pn = [jnp.exp(a_n[h] - a_m[h]) for h in heads]
    m_p = [jnp.exp2(m_s[h] - jnp.max(m_s[h], axis=-1, keepdims=True)) for h in heads]
    l_n = [jnp.where(before, log_one_minus(c_n[h]), 0.0) for h in heads]
    l_c = [[log_one_minus(c_c[h][j]) for j in subs] for h in heads]
    for h in heads:
        pv = (_dot_nt(a_pc[h].astype(BF16), av_t_ref[0, h].astype(BF16))
              + _dot(a_pn[h].astype(BF16), av_n_ref[0, h]))
        oa_ref[0, h] = _head_rms(pv / (jnp.sum(a_pc[h], axis=-1, keepdims=True)
                                       + jnp.sum(a_pn[h], axis=-1, keepdims=True)))
        om_ref[0, h] = _head_rms(_dot_nt(m_p[h].astype(BF16), mv_t_ref[0, h].astype(BF16))
                                 / jnp.sum(m_p[h], axis=-1, keepdims=True))
    t_n = [_dot(l_n[h].astype(BF16), tri_ref[0:t, 0:t]) for h in heads]
    t_c = [[_dot(l_c[h][j].astype(BF16), tri_ref[...]) for j in subs] for h in heads]
    w_n = [jnp.exp2(jnp.where(before, l_n[h] - c_n[h] + t_n[h], NEG)).astype(BF16) for h in heads]
    w_c = [[jnp.exp2(l_c[h][j] - c_c[h][j] + t_c[h][j]).astype(BF16) for j in subs] for h in heads]
    pv_n = [_dot(w_n[h], cv_n_ref[0, h]) for h in heads]
    pv_c = [[_dot_nt(w_c[h][j], cv_t_ref[0, h, :, j * SUB:(j + 1) * SUB].astype(BF16)) for j in subs]
            for h in heads]
    for h in heads:
        acc = pv_n[h]
        run = jnp.sum(l_n[h], axis=-1, keepdims=True)
        for j in reversed(subs):
            acc = acc + pv_c[h][j] * jnp.exp2(run)
            run = run + jnp.sum(l_c[h][j], axis=-1, keepdims=True)
        oc_ref[0, h] = _head_rms(acc)


def _step_attn(pr, layer, ak_t, av_t, ck_t, cv_t, mk_t, mv_t, bias):
    n, _, t, _ = pr['aq'].shape
    assert ck_t.shape[-1] % SUB == 0
    r = jnp.arange(SUB)
    tri = (r[:, None] > r[None, :]).astype(BF16)
    ops = (pr['aq'], pr['cq'], pr['mq'], ak_t, av_t, pr['ak'], pr['av'], ck_t, cv_t, pr['ck'], pr['cv'], mk_t, mv_t)

    def row_spec(a):
        if len(a.shape) == 5:
            return pl.BlockSpec((None, 1) + a.shape[2:], lambda i: (layer, i, 0, 0, 0))
        return pl.BlockSpec((1,) + a.shape[1:], lambda i: (i, 0, 0, 0))

    out = jax.ShapeDtypeStruct((n, HEADS, t, HEAD_DIM), F32)
    return pl.pallas_call(
        functools.partial(_step_attn_kernel, t=t),
        grid=(n,),
        in_specs=[row_spec(a) for a in ops]
        + [pl.BlockSpec(bias.shape, lambda i: (0, 0, 0)), pl.BlockSpec((SUB, SUB), lambda i: (0, 0))],
        out_specs=[row_spec(out)] * 3,
        out_shape=[out] * 3,
        compiler_params=_params(1),
        name='step_attn',
    )(*ops, bias, tri)


def _merge_kernel(x_ref, oa_ref, ob_ref, oc_ref, om_ref, gate_ref, og_ref, w_ref, y_ref):
    acc = x_ref[0]
    for gi, o_ref in enumerate((oa_ref, ob_ref, oc_ref, om_ref)):
        for pair in range(HEADS // 2):
            j = HEADS * gi + 2 * pair
            ys = [o_ref[0, 2 * pair + i] * og_ref[j + i] * gate_ref[0, j + i] for i in range(2)]
            acc = acc + _dot(jnp.concatenate(ys, axis=-1).astype(BF16), w_ref[j // 2])
    y_ref[0] = acc


def _merge(x, o_a, o_b, o_c, o_m, gates, lw, tm):
    n, t, _ = x.shape
    hspec = pl.BlockSpec((1, HEADS, tm, HEAD_DIM), lambda i, j: (i, 0, j, 0))
    return pl.pallas_call(
        _merge_kernel,
        grid=(n, t // tm),
        in_specs=[pl.BlockSpec((1, tm, D_MODEL), lambda i, j: (i, j, 0)), hspec, hspec, hspec, hspec,
                  pl.BlockSpec((1, 4 * HEADS, tm, HEAD_DIM), lambda i, j: (i, 0, j, 0)),
                  pl.BlockSpec((4 * HEADS, 1, HEAD_DIM), lambda i, j: (0, 0, 0)),
                  pl.BlockSpec((2 * HEADS, 2 * HEAD_DIM, D_MODEL), lambda i, j: (0, 0, 0))],
        out_specs=pl.BlockSpec((1, tm, D_MODEL), lambda i, j: (i, j, 0)),
        out_shape=jax.ShapeDtypeStruct((n, t, D_MODEL), F32),
        compiler_params=_params(2),
        name='merge',
    )(x, o_a, o_b, o_c, o_m, gates, lw['out_g'], lw['w_out'])


def _block_diag(block, reps):
    return jnp.kron(jnp.eye(reps, dtype=F32), block)


def _layer_weights(l, p):
    w_in = p['w_in'][l]
    zeros = lambda c: jnp.zeros((D_MODEL, c), F32)
    kr0 = 1408
    w_in = jnp.concatenate([w_in[:, :kr0], zeros(ROPE_LANE0), w_in[:, kr0:kr0 + B_ROPE],
                            zeros(LANES - ROPE_LANE0 - B_ROPE), w_in[:, kr0 + B_ROPE:]], axis=1)
    assert w_in.shape[1] == IN_WIDTH_PADDED
    ones = lambda k: jnp.full((k, k), 1.0 / k, F32)
    zpad = lambda k: jnp.zeros((k, k), F32)
    tile = lambda g, reps=HEADS: jnp.tile(g, reps)[None, :]
    pad_to = lambda g, k: jnp.concatenate([g, jnp.zeros((k - g.shape[0],), F32)])
    wq = jnp.pad(p['b_wq_b'][l].reshape(Q_LORA, HEADS, B_QK), ((0, 0), (0, 0), (0, LANES - B_QK)))
    wkv = p['b_wkv_b'][l].reshape(KV_LORA, HEADS, B_NOPE + HEAD_DIM)
    wk = jnp.pad(wkv[:, :, :B_NOPE], ((0, 0), (0, 0), (0, LANES - B_NOPE))).reshape(KV_LORA, HEADS * LANES)
    wv = jnp.pad(wkv[:, :, B_NOPE:], ((0, 0), (0, 0), (0, LANES - HEAD_DIM))).reshape(KV_LORA, HEADS * LANES)
    gq128 = jax.scipy.linalg.block_diag(ones(B_NOPE), ones(B_ROPE), zpad(LANES - B_QK))
    gk128 = jax.scipy.linalg.block_diag(ones(B_NOPE), zpad(LANES - B_NOPE))
    place = jnp.zeros((B_ROPE, LANES), F32).at[jnp.arange(B_ROPE), ROPE_LANE0 + jnp.arange(B_ROPE)].set(1.0)
    return dict(
        norm_g=p['norm_g'][l][None, :],
        w_in=w_in.astype(BF16),
        g64=_block_diag(ones(HEAD_DIM), HEADS).astype(BF16),
        a_qn_g=tile(p['a_qn_g'][l]), a_kn_g=tile(p['a_kn_g'][l]),
        b_cq_g=p['b_cq_g'][l][None, :],
        wq=wq.reshape(Q_LORA, HEADS * LANES).astype(BF16),
        gq=_block_diag(gq128, 2).astype(BF16),
        gq_gain=tile(jnp.concatenate([p['b_qn_g'][l], pad_to(p['b_qr_g'][l], LANES - B_NOPE)])),
        b_ckv_g=p['b_ckv_g'][l][None, :],
        kr_gain=jnp.concatenate([jnp.zeros((ROPE_LANE0,), F32), pad_to(p['b_kr_g'][l], LANES - ROPE_LANE0)])[None, :],
        m_qn_g=tile(p['m_qn_g'][l]),
        wkv=jnp.concatenate([wk, wv], axis=1).astype(BF16),
        place=place.astype(BF16),
        gk=_block_diag(gk128, 2).astype(BF16),
        gk_gain=tile(pad_to(p['b_kn_g'][l], LANES)),
        v_one=jnp.zeros((1, LANES), F32).at[0, HEAD_DIM].set(1.0),
        m_norm_g=p['m_norm_g'][l][None, :],
        w_mem_kv=p['w_mem_kv'][l].astype(BF16),
        m_kn_g=tile(p['m_kn_g'][l]),
        out_g=p['out_g'][l].reshape(4 * HEADS, 1, HEAD_DIM),
        w_out=p['w_out'][l].reshape(2 * HEADS, 2 * HEAD_DIM, D_MODEL).astype(BF16),
        a_rel_bias=p['a_rel_bias'][l],
    )


def _rope_tables(pos):
    half = B_ROPE // 2
    freqs = ROPE_THETA ** (-jnp.arange(half, dtype=F32) / half)
    ang = pos.astype(F32)[:, None] * freqs[None, :]
    cos, sin = jnp.cos(ang), jnp.sin(ang)
    t = pos.shape[0]
    one, zero = jnp.ones((t, ROPE_LANE0), F32), jnp.zeros((t, ROPE_LANE0), F32)
    tail, z16 = jnp.zeros((t, LANES - ROPE_LANE0 - B_ROPE), F32), jnp.zeros((t, half), F32)
    c = jnp.concatenate([one, cos, cos, tail], axis=1)
    s1 = jnp.concatenate([zero, -sin, z16, tail], axis=1)
    s2 = jnp.concatenate([zero, z16, sin, tail], axis=1)
    return c, s1, s2


def _head_major(a):
    return a.transpose(0, 2, 1, 3)


def _token_major(a):
    return a.reshape(a.shape[0], a.shape[1], HEADS, HEAD_DIM)


def _pad_rows(a, front, back):
    return jnp.pad(a, ((0, 0), (0, 0), (front, back), (0, 0)))


def _tiles(s):
    tm = min(512, s)
    tq = min(PROMPT_TQ, s)
    g = min(4, s // CHUNK)
    return tm, tq, g


def _layer_prompt(x, mem, lw, rope_tabs):
    n, s, _ = x.shape
    tm, tq, g = _tiles(s)
    pr = _proj(x, rope_tabs, lw, tm)
    kb, vb = _kv_expand(pr['ckv'], pr['kr'], lw, min(1024, s))
    bias = _band_bias(lw['a_rel_bias'], g)
    o_a = _band_attn(pr['aq'], pr['ak'], pr['av'], bias, g, 0)
    o_b = _softmax_attn(pr['bq'], kb, vb, hb=PROMPT_HB, tq=tq, tsup=tq, dw=tq, causal=True)
    o_c = _stick_attn(pr['cq'], pr['ck'], pr['cv'], hb=PROMPT_HB, tq=tq, tsup=tq, dw=tq)
    mk, mv = _mem_kv(mem, lw)
    mk4, mv4 = _token_major(mk), _token_major(mv)
    o_m = _softmax_attn(pr['mq'], _head_major(mk4).astype(BF16), _augment_v(_head_major(mv4).astype(BF16)),
                        hb=HEADS, tq=tq, tsup=mk.shape[1], causal=False)
    y = _merge(x, o_a, o_b, o_c, o_m, pr['gates'], lw, tm)
    keep = min(A_WIN, s)
    state = (_token_major(pr['ak_s'][:, s - keep:]), _token_major(pr['av_s'][:, s - keep:]), pr['ckv'], pr['kr'],
             _token_major(pr['ck_s']), _token_major(pr['cv_s']), mk4, mv4)
    return y, state


def _layer_step(x, lw, rope_tabs, layer, cb_ckv, cb_kr, caches_t):
    n, t, _ = x.shape
    n_past = cb_ckv.shape[1]
    assert t == CHUNK and n_past % CHUNK == 0 and caches_t[0].shape[-1] == A_WIN
    dw = 2 * CHUNK
    back = dw - t
    pr = _proj(x, rope_tabs, lw, t)
    ckv_all = jnp.pad(jnp.concatenate([cb_ckv, pr['ckv']], 1), ((0, 0), (0, back), (0, 0)))
    kr_all = jnp.pad(jnp.concatenate([cb_kr, pr['kr']], 1), ((0, 0), (0, back), (0, 0)))
    kb, vb = _kv_expand(ckv_all, kr_all, lw, ckv_all.shape[1])
    o_b = _softmax_attn(pr['bq'], kb, vb, hb=HEADS, tq=t, tsup=n_past, dw=dw, q0=n_past, causal=True)
    o_a, o_c, o_m = _step_attn(pr, layer, *caches_t, _band_bias(lw['a_rel_bias'], 1))
    y = _merge(x, o_a, o_b, o_c, o_m, pr['gates'], lw, t)
    state = (_token_major(pr['ak_s']), _token_major(pr['av_s']), pr['ckv'], pr['kr'],
             _token_major(pr['ck_s']), _token_major(pr['cv_s']))
    return y, state


def kernel(x_prompt, x_sample, mem_prompt, cache_a_k, cache_a_v, cache_b_ckv, cache_b_krope, cache_c_k, cache_c_v,
           cache_mem_k, cache_mem_v, norm_g, w_in, a_qn_g, a_kn_g, a_rel_bias, b_cq_g, b_wq_b, b_ckv_g, b_wkv_b,
           b_qn_g, b_qr_g, b_kn_g, b_kr_g, m_norm_g, w_mem_kv, m_qn_g, m_kn_g, out_g, w_out):
    p = dict(norm_g=norm_g, w_in=w_in, a_qn_g=a_qn_g, a_kn_g=a_kn_g, a_rel_bias=a_rel_bias, b_cq_g=b_cq_g,
             b_wq_b=b_wq_b, b_ckv_g=b_ckv_g, b_wkv_b=b_wkv_b, b_qn_g=b_qn_g, b_qr_g=b_qr_g, b_kn_g=b_kn_g,
             b_kr_g=b_kr_g, m_norm_g=m_norm_g, w_mem_kv=w_mem_kv, m_qn_g=m_qn_g, m_kn_g=m_kn_g, out_g=out_g,
             w_out=w_out)
    depth = w_in.shape[0]
    s, t, n_past = x_prompt.shape[1], x_sample.shape[1], cache_b_ckv.shape[2]
    tabs_p = _rope_tables(jnp.arange(s))
    tabs_s = _rope_tables(n_past + jnp.arange(t))
    hp, hs = x_prompt, x_sample
    caches_t = tuple(c.transpose(0, 1, 3, 4, 2)
                     for c in (cache_a_k, cache_a_v, cache_c_k, cache_c_v, cache_mem_k, cache_mem_v))
    p_states, s_states = [], []
    for l in range(depth):
        lw = _layer_weights(l, p)
        hp, sp = _layer_prompt(hp, mem_prompt, lw, tabs_p)
        hs, ss = _layer_step(hs, lw, tabs_s, l, cache_b_ckv[l], cache_b_krope[l], caches_t)
        p_states.append(sp)
        s_states.append(ss)
    outs_p = [jnp.stack([st[i] for st in p_states]) for i in range(8)]
    outs_s = [jnp.stack([st[i] for st in s_states]) for i in range(6)]
    outs_s[0] = jnp.concatenate([cache_a_k[:, :, t:], outs_s[0]], axis=2)
    outs_s[1] = jnp.concatenate([cache_a_v[:, :, t:], outs_s[1]], axis=2)
    return (hp, hs, *outs_p, *outs_s)
```
